```python
import math
import jax, jax.numpy as jnp
from jax import lax
import numpy as np


D_MODEL = 1024
BATCH = 2
SEQ = 8192
DEPTH = 4

D_MIX = D_MODEL
SSD_WIDTH = D_MIX // 2
SSD_HEADDIM = 64
SSD_HEADS = SSD_WIDTH // SSD_HEADDIM
SSD_GROUPS = 2
SSD_HEADS_PER_GROUP = SSD_HEADS // SSD_GROUPS
SSD_STATE = 128
SSD_XBC = SSD_WIDTH + 2 * SSD_GROUPS * SSD_STATE
CONV_WIDTH = 4
CHUNK = 128
ATTN_WIDTH = D_MIX - SSD_WIDTH
DIFF_HEAD_DIM = 64
DIFF_HEADS = ATTN_WIDTH // (2 * DIFF_HEAD_DIM)
DIFF_V_DIM = 2 * DIFF_HEAD_DIM
ATTN_QK = DIFF_HEADS * 2 * DIFF_HEAD_DIM
ATTN_V = DIFF_HEADS * DIFF_V_DIM
Q_BLOCK = 128
ROPE_THETA = 10000.0
D_FF = -(-(8 * D_MODEL) // (3 * 256)) * 256
IN_SPLITS = (SSD_WIDTH, SSD_XBC, SSD_HEADS, ATTN_QK, ATTN_QK, ATTN_V)
D_IN_PROJ = SSD_WIDTH + SSD_XBC + SSD_HEADS + 2 * ATTN_QK + ATTN_V
ALPHA = (2 * DEPTH) ** 0.25
BETA = (8 * DEPTH) ** -0.25
N_MOD = 6
EPS = 1e-5

kernel_name = 'hymba_ssd_diffattn_deepnorm_adaln'


def layer_norm(x, g, b):
    xf = x.astype(jnp.float32)
    mu = jnp.mean(xf, -1, keepdims=True)
    var = jnp.mean(jnp.square(xf - mu), -1, keepdims=True)
    return ((xf - mu) * lax.rsqrt(var + EPS) * g + b).astype(x.dtype)


def rms_norm(x, g):
    xf = x.astype(jnp.float32)
    return (xf * lax.rsqrt(jnp.mean(xf * xf, -1, keepdims=True) + EPS) * g).astype(x.dtype)


def rope_tables(seq, dim):
    inv = 1.0 / (ROPE_THETA ** (jnp.arange(0, dim, 2, dtype=jnp.float32) / dim))
    ang = jnp.arange(seq, dtype=jnp.float32)[:, None] * inv[None, :]
    ang = jnp.concatenate([ang, ang], -1)
    return jnp.cos(ang), jnp.sin(ang)


def apply_rope(x, cos, sin):
    half = x.shape[-1] // 2
    rot = jnp.concatenate([-x[..., half:], x[..., :half]], -1)
    shape = (1, cos.shape[0]) + (1,) * (x.ndim - 3) + (cos.shape[1],)
    return x * cos.reshape(shape) + rot * sin.reshape(shape)


def causal_depthwise_conv(u, w, b):
    k = w.shape[0]
    out = lax.conv_general_dilated(
        u, w[:, None, :].astype(u.dtype), window_strides=(1,), padding=((k - 1, 0),),
        dimension_numbers=('NWC', 'WIO', 'NWC'), feature_group_count=u.shape[-1])
    return out + b


def segsum_exp(a):
    cs = jnp.cumsum(a, -1)
    n = a.shape[-1]
    diff = cs[..., :, None] - cs[..., None, :]
    mask = jnp.tril(jnp.ones((n, n), dtype=bool))
    return jnp.exp(jnp.where(mask, diff, -jnp.inf))


def ssd_chunked(xs, a, bm, cm):
    f32 = jnp.float32
    xs, a, bm, cm = xs.astype(f32), a.astype(f32), bm.astype(f32), cm.astype(f32)
    b, s, g, e, p = xs.shape
    n = bm.shape[-1]
    c = s // CHUNK
    xs = xs.reshape(b, c, CHUNK, g, e, p)
    bm = bm.reshape(b, c, CHUNK, g, n)
    cm = cm.reshape(b, c, CHUNK, g, n)
    a = a.reshape(b, c, CHUNK, g, e).transpose(0, 3, 4, 1, 2)
    a_cs = jnp.cumsum(a, -1)
    lmat = segsum_exp(a)
    cb = jnp.einsum('bclgn,bcsgn->bgcls', cm, bm)
    y_diag = jnp.einsum('bgcls,bgecls,bcsgep->bclgep', cb, lmat, xs)
    decay_states = jnp.exp(a_cs[..., -1:] - a_cs)
    states = jnp.einsum('bclgn,bgecl,bclgep->bcgepn', bm, decay_states, xs)
    chunk_decay = jnp.exp(a_cs[..., -1])

    def step(h, inp):
        st, dec = inp
        return dec[..., None, None] * h + st, h

    h0 = jnp.zeros((b, g, e, p, n), f32)
    _, prev = lax.scan(step, h0, (states.transpose(1, 0, 2, 3, 4, 5),
                                  chunk_decay.transpose(3, 0, 1, 2)))
    prev = prev.transpose(1, 0, 2, 3, 4, 5)
    y_off = jnp.einsum('bclgn,bcgepn,bgecl->bclgep', cm, prev, jnp.exp(a_cs))
    return (y_diag + y_off).reshape(b, s, g, e, p)


def diff_attention(q, k, v, lam):
    b, s, nh, _, d = q.shape
    nblk = s // Q_BLOCK
    qb = q.reshape(b, nblk, Q_BLOCK, nh, 2, d).transpose(1, 0, 2, 3, 4, 5)
    key_pos = jnp.arange(s)

    def block(args):
        qi, i = args
        scores = jnp.einsum('bqhmd,bkhmd->bhmqk', qi, k, preferred_element_type=jnp.float32)
        q_pos = i * Q_BLOCK + jnp.arange(Q_BLOCK)
        mask = key_pos[None, :] <= q_pos[:, None]
        prob = jax.nn.softmax(jnp.where(mask, scores, -jnp.inf), axis=-1)
        w = prob[:, :, 0] - lam * prob[:, :, 1]
        return jnp.einsum('bhqk,bkhv->bqhv', w.astype(v.dtype), v)

    out = lax.map(block, (qb, jnp.arange(nblk)))
    return out.transpose(1, 0, 2, 3, 4).reshape(b, s, nh, v.shape[-1])


def hybrid_mixer(h, w_in, conv_w, conv_b, dt_bias, a_log, d_skip, ssd_norm_w,
                 lam_qk, attn_norm_w, w_out, lambda_init, cos, sin):
    b, s, _ = h.shape
    proj = h @ w_in
    offs = np.cumsum(IN_SPLITS)[:-1].tolist()
    z, xbc, dt, q, k, v = jnp.split(proj, offs, axis=-1)

    xbc = jax.nn.silu(causal_depthwise_conv(xbc, conv_w, conv_b))
    xs, bm, cm = jnp.split(xbc, [SSD_WIDTH, SSD_WIDTH + SSD_GROUPS * SSD_STATE], axis=-1)
    xs = xs.reshape(b, s, SSD_GROUPS, SSD_HEADS_PER_GROUP, SSD_HEADDIM)
    bm = bm.reshape(b, s, SSD_GROUPS, SSD_STATE)
    cm = cm.reshape(b, s, SSD_GROUPS, SSD_STATE)
    dt = jax.nn.softplus(dt.astype(jnp.float32) + dt_bias.astype(jnp.float32))
    dt = dt.reshape(b, s, SSD_GROUPS, SSD_HEADS_PER_GROUP)
    a_head = -jnp.exp(a_log.astype(jnp.float32)).reshape(SSD_GROUPS, SSD_HEADS_PER_GROUP)
    y = ssd_chunked(xs * dt[..., None], a_head * dt, bm, cm)
    y = y + d_skip.reshape(SSD_GROUPS, SSD_HEADS_PER_GROUP)[:, :, None] * xs
    y = y.reshape(b, s, SSD_WIDTH) * jax.nn.silu(z.astype(jnp.float32))
    y = rms_norm(y.reshape(b, s, SSD_GROUPS, SSD_WIDTH // SSD_GROUPS),
                 ssd_norm_w.reshape(SSD_GROUPS, SSD_WIDTH // SSD_GROUPS))
    y_ssd = y.reshape(b, s, SSD_WIDTH).astype(h.dtype)

    q = q.reshape(b, s, DIFF_HEADS, 2, DIFF_HEAD_DIM)
    k = k.reshape(b, s, DIFF_HEADS, 2, DIFF_HEAD_DIM)
    v = v.reshape(b, s, DIFF_HEADS, DIFF_V_DIM)
    q = (apply_rope(q, cos, sin) * (DIFF_HEAD_DIM ** -0.5)).astype(h.dtype)
    k = apply_rope(k, cos, sin).astype(h.dtype)
    lq = lam_qk.astype(jnp.float32)
    lam = jnp.exp(jnp.sum(lq[0] * lq[1])) - jnp.exp(jnp.sum(lq[2] * lq[3])) + lambda_init
    o = diff_attention(q, k, v, lam)
    o = rms_norm(o, attn_norm_w) * (1.0 - lambda_init)
    y_attn = o.reshape(b, s, ATTN_WIDTH).astype(h.dtype)

    return jnp.concatenate([y_ssd, y_attn], axis=-1) @ w_out


def swiglu(h, w_gate_up, w_down):
    g, u = jnp.split(h @ w_gate_up, 2, axis=-1)
    return (jax.nn.silu(g) * u) @ w_down


def setup_inputs(seed: int = 0) -> dict:
    key = jax.random.key(seed)
    ks = jax.random.split(key, 22)
    f32 = jnp.float32
    L = DEPTH

    def nrm(k, shape, scale):
        return jax.random.normal(k, shape, f32) * scale

    x = nrm(ks[0], (BATCH, SEQ, D_MODEL), 1.0)
    c = nrm(ks[1], (BATCH, D_MODEL), 1.0)
    w_mod = nrm(ks[2], (L, D_MODEL, N_MOD * D_MODEL), 0.01)
    b_mod = nrm(ks[3], (L, N_MOD * D_MODEL), 0.01)
    w_in = nrm(ks[4], (L, D_MODEL, D_IN_PROJ), D_MODEL ** -0.5)
    conv_w = nrm(ks[5], (L, CONV_WIDTH, SSD_XBC), CONV_WIDTH ** -0.5)
    conv_b = nrm(ks[6], (L, SSD_XBC), 0.02)
    dt0 = jnp.exp(jax.random.uniform(ks[7], (L, SSD_HEADS), f32, math.log(1e-3), math.log(1e-1)))
    dt_bias = dt0 + jnp.log(-jnp.expm1(-dt0))
    a_log = jnp.log(jax.random.uniform(ks[8], (L, SSD_HEADS), f32, 1.0, 16.0))
    d_skip = 1.0 + nrm(ks[9], (L, SSD_HEADS), 0.02)
    ssd_norm_w = 1.0 + nrm(ks[10], (L, SSD_WIDTH), 0.02)
    lam_qk = nrm(ks[11], (L, 4, DIFF_HEAD_DIM), 0.1)
    attn_norm_w = 1.0 + nrm(ks[12], (L, DIFF_V_DIM), 0.02)
    w_out = nrm(ks[13], (L, D_MIX, D_MODEL), BETA * D_MIX ** -0.5)
    ln1_g = 1.0 + nrm(ks[14], (L, D_MODEL), 0.02)
    ln1_b = nrm(ks[15], (L, D_MODEL), 0.02)
    w_gate_up = nrm(ks[16], (L, D_MODEL, 2 * D_FF), D_MODEL ** -0.5)
    w_down = nrm(ks[17], (L, D_FF, D_MODEL), BETA * D_FF ** -0.5)
    ln2_g = 1.0 + nrm(ks[18], (L, D_MODEL), 0.02)
    ln2_b = nrm(ks[19], (L, D_MODEL), 0.02)
    return {'x': x, 'c': c, 'w_mod': w_mod, 'b_mod': b_mod, 'w_in': w_in,
            'conv_w': conv_w, 'conv_b': conv_b, 'dt_bias': dt_bias, 'a_log': a_log,
            'd_skip': d_skip, 'ssd_norm_w': ssd_norm_w, 'lam_qk': lam_qk,
            'attn_norm_w': attn_norm_w, 'w_out': w_out, 'ln1_g': ln1_g, 'ln1_b': ln1_b,
            'w_gate_up': w_gate_up, 'w_down': w_down, 'ln2_g': ln2_g, 'ln2_b': ln2_b}


def reference(x, c, w_mod, b_mod, w_in, conv_w, conv_b, dt_bias, a_log, d_skip,
              ssd_norm_w, lam_qk, attn_norm_w, w_out, ln1_g, ln1_b, w_gate_up,
              w_down, ln2_g, ln2_b):
    cos, sin = rope_tables(x.shape[1], DIFF_HEAD_DIM)
    cond = jax.nn.silu(c)
    for l in range(DEPTH):
        mod = (cond @ w_mod[l] + b_mod[l])[:, None, :]
        shift1, scale1, gate1, shift2, scale2, gate2 = jnp.split(mod, N_MOD, axis=-1)
        lambda_init = 0.8 - 0.6 * math.exp(-0.3 * l)
        h = x * (1.0 + scale1) + shift1
        y = hybrid_mixer(h, w_in[l], conv_w[l], conv_b[l], dt_bias[l], a_log[l], d_skip[l],
                         ssd_norm_w[l], lam_qk[l], attn_norm_w[l], w_out[l],
                         lambda_init, cos, sin)
        x = layer_norm(ALPHA * x + (1.0 + gate1) * y, ln1_g[l], ln1_b[l])
        h = x * (1.0 + scale2) + shift2
        y = swiglu(h, w_gate_up[l], w_down[l])
        x = layer_norm(ALPHA * x + (1.0 + gate2) * y, ln2_g[l], ln2_b[l])
    return x
```

```python
import functools
import math

import jax
import jax.numpy as jnp
from jax import lax
from jax.experimental import pallas as pl
from jax.experimental.pallas import tpu as pltpu

F32 = jnp.float32
BF16 = jnp.bfloat16

D_MODEL = 1024
SSD_WIDTH = 512
SSD_HEADDIM = 64
SSD_HEADS = 8
SSD_GROUPS = 2
SSD_HEADS_PER_GROUP = 4
SSD_STATE = 128
SSD_XBC = SSD_WIDTH + 2 * SSD_GROUPS * SSD_STATE
CONV_WIDTH = 4
CHUNK = 128
DIFF_HEAD_DIM = 64
DIFF_HEADS = 4
DIFF_V_DIM = 128
ATTN_QK = 512
ATTN_V = 512
ROPE_THETA = 10000.0
D_FF = 2816
N_MOD = 6
EPS = 1e-5

LANES = 128
SUBLANES = 8
VMEM_LIMIT_BYTES = 56 * 1024 * 1024

ZX_W = SSD_WIDTH + SSD_XBC
Q_OFF = ZX_W
K_OFF = Q_OFF + ATTN_QK
V_OFF = K_OFF + ATTN_QK
DT_OFF = V_OFF + ATTN_V
IN_PACKED = DT_OFF + LANES

ROW_TILE = 512
ATTN_TILE = 512
FF_TILE = 1408


def _sigmoid(x):
    return 1.0 / (1.0 + jnp.exp(-x))


def _params(*sem):
    return pltpu.CompilerParams(dimension_semantics=sem, vmem_limit_bytes=VMEM_LIMIT_BYTES)


def _mod_kernel(c_ref, w_ref, b_ref, o_ref):
    c = c_ref[...]
    cond = (c * _sigmoid(c)).astype(BF16)
    o_ref[0] = jnp.dot(cond, w_ref[0].astype(BF16), preferred_element_type=F32) + b_ref[0]


def _modulation(c_pad, w_mod, b_mod):
    depth, d, n = w_mod.shape
    tn = 2048
    return pl.pallas_call(
        _mod_kernel,
        grid=(depth, n // tn),
        in_specs=[
            pl.BlockSpec((SUBLANES, d), lambda l, j: (0, 0)),
            pl.BlockSpec((1, d, tn), lambda l, j: (l, 0, j)),
            pl.BlockSpec((1, 1, tn), lambda l, j: (l, 0, j)),
        ],
        out_specs=pl.BlockSpec((1, SUBLANES, tn), lambda l, j: (l, 0, j)),
        out_shape=jax.ShapeDtypeStruct((depth, SUBLANES, n), F32),
        compiler_params=_params("parallel", "parallel"),
    )(c_pad, w_mod, b_mod.reshape(depth, 1, n))


def _inproj_kernel(x_ref, sc_ref, sh_ref, w_ref, cos_ref, sin_ref,
                   zx_ref, dt_ref, q_ref, k_ref, v_ref):
    h = (x_ref[...] * (1.0 + sc_ref[0]) + sh_ref[0]).astype(BF16)
    zx_ref[...] = jnp.dot(h, w_ref[:, 0:ZX_W], preferred_element_type=F32)
    cos = cos_ref[...]
    sin = sin_ref[...]
    lane = lax.broadcasted_iota(jnp.int32, cos.shape, 1)
    first_half = (lane % DIFF_HEAD_DIM) < (DIFF_HEAD_DIM // 2)

    def rope(t):
        rot = jnp.where(first_half, pltpu.roll(t, LANES - DIFF_HEAD_DIM // 2, 1),
                        pltpu.roll(t, DIFF_HEAD_DIM // 2, 1))
        return t * cos + rot * sin

    for hd in range(DIFF_HEADS):
        lo = hd * LANES
        qh = jnp.dot(h, w_ref[:, Q_OFF + lo:Q_OFF + lo + LANES], preferred_element_type=F32)
        q_ref[:, lo:lo + LANES] = (rope(qh) * (DIFF_HEAD_DIM ** -0.5)).astype(BF16)
        kh = jnp.dot(h, w_ref[:, K_OFF + lo:K_OFF + lo + LANES], preferred_element_type=F32)
        k_ref[:, lo:lo + LANES] = rope(kh).astype(BF16)
    v_ref[...] = jnp.dot(h, w_ref[:, V_OFF:V_OFF + ATTN_V], preferred_element_type=F32).astype(BF16)
    dt_ref[...] = jnp.dot(h, w_ref[:, DT_OFF:DT_OFF + LANES], preferred_element_type=F32)


def _inproj(x2, mod3, w_packed, cos2, sin2, seq):
    t, d = x2.shape
    tm = ROW_TILE
    per_seq = seq // tm
    row = lambda i: (i, 0)
    return pl.pallas_call(
        _inproj_kernel,
        grid=(t // tm,),
        in_specs=[
            pl.BlockSpec((tm, d), row),
            pl.BlockSpec((1, 1, d), lambda i: ((i // per_seq) * N_MOD + 1, 0, 0)),
            pl.BlockSpec((1, 1, d), lambda i: ((i // per_seq) * N_MOD + 0, 0, 0)),
            pl.BlockSpec((d, IN_PACKED), lambda i: (0, 0)),
            pl.BlockSpec((tm, LANES), lambda i: (i % per_seq, 0)),
            pl.BlockSpec((tm, LANES), lambda i: (i % per_seq, 0)),
        ],
        out_specs=[
            pl.BlockSpec((tm, ZX_W), row),
            pl.BlockSpec((tm, LANES), row),
            pl.BlockSpec((tm, ATTN_QK), row),
            pl.BlockSpec((tm, ATTN_QK), row),
            pl.BlockSpec((tm, ATTN_V), row),
        ],
        out_shape=[
            jax.ShapeDtypeStruct((t, ZX_W), F32),
            jax.ShapeDtypeStruct((t, LANES), F32),
            jax.ShapeDtypeStruct((t, ATTN_QK), BF16),
            jax.ShapeDtypeStruct((t, ATTN_QK), BF16),
            jax.ShapeDtypeStruct((t, ATTN_V), BF16),
        ],
        compiler_params=_params("parallel"),
    )(x2, mod3, mod3, w_packed, cos2, sin2)


def _ssd_kernel(zx_ref, dt_ref, cw_ref, cb_ref, dtb_ref, alog_ref, dsk_ref, nw_ref,
                y_ref, xpad, state, ybuf):
    L = CHUNK
    c = pl.program_id(1)

    @pl.when(c == 0)
    def _():
        xpad[...] = jnp.zeros_like(xpad)
        state[...] = jnp.zeros_like(state)

    xpad[0:SUBLANES, :] = xpad[L:L + SUBLANES, :]
    xpad[SUBLANES:SUBLANES + L, :] = zx_ref[:, SSD_WIDTH:ZX_W]
    conv = cb_ref[...]
    for kk in range(CONV_WIDTH):
        off = SUBLANES - (CONV_WIDTH - 1) + kk
        conv = conv + cw_ref[kk:kk + 1, :] * xpad[off:off + L, :]
    xc = conv * _sigmoid(conv)
    xs = xc[:, 0:SSD_WIDTH]

    pre = dt_ref[...] + dtb_ref[...]
    dt = jnp.maximum(pre, 0.0) + jnp.log(1.0 + jnp.exp(-jnp.abs(pre)))
    a = dt * (-jnp.exp(alog_ref[...]))
    row = lax.broadcasted_iota(jnp.int32, (L, L), 0)
    col = lax.broadcasted_iota(jnp.int32, (L, L), 1)
    causal = row >= col
    cs = jnp.dot(causal.astype(F32), a, precision=lax.Precision.HIGHEST,
                 preferred_element_type=F32)
    cs_t = cs.T

    for g in range(SSD_GROUPS):
        bm = xc[:, SSD_WIDTH + g * SSD_STATE:SSD_WIDTH + (g + 1) * SSD_STATE]
        cm = xc[:, SSD_WIDTH + (SSD_GROUPS + g) * SSD_STATE:SSD_WIDTH + (SSD_GROUPS + g + 1) * SSD_STATE]
        cm_b = cm.astype(BF16)
        cb = lax.dot_general(cm_b, bm.astype(BF16), (((1,), (1,)), ((), ())),
                             preferred_element_type=F32)
        bm_t = bm.T
        for e in range(SSD_HEADS_PER_GROUP):
            hd = g * SSD_HEADS_PER_GROUP + e
            cs_col = cs[:, hd:hd + 1]
            cs_row = cs_t[hd:hd + 1, :]
            cs_last = cs_t[hd:hd + 1, L - 1:L]
            lmat = jnp.exp(jnp.where(causal, cs_col - cs_row, -jnp.inf))
            xdt = (xs[:, hd * SSD_HEADDIM:(hd + 1) * SSD_HEADDIM] * dt[:, hd:hd + 1]).astype(BF16)
            y_diag = jnp.dot((cb * lmat).astype(BF16), xdt, preferred_element_type=F32)
            prev = state[hd]
            y_off = jnp.dot(cm_b, prev.astype(BF16), preferred_element_type=F32) * jnp.exp(cs_col)
            decay = jnp.exp(cs_last - cs_row)
            st = jnp.dot((bm_t * decay).astype(BF16), xdt, preferred_element_type=F32)
            state[hd] = jnp.exp(cs_last) * prev + st
            ybuf[:, hd * SSD_HEADDIM:(hd + 1) * SSD_HEADDIM] = y_diag + y_off

    z = zx_ref[:, 0:SSD_WIDTH]
    y = (ybuf[...] + dsk_ref[...] * xs) * (z * _sigmoid(z))
    gw = SSD_WIDTH // SSD_GROUPS
    for g in range(SSD_GROUPS):
        yg = y[:, g * gw:(g + 1) * gw]
        ms = jnp.mean(yg * yg, axis=-1, keepdims=True)
        y_ref[:, g * gw:(g + 1) * gw] = (yg * lax.rsqrt(ms + EPS)
                                         * nw_ref[:, g * gw:(g + 1) * gw]).astype(BF16)


def _ssd(zx, dtp, conv_w8, conv_b, dtb, alog, dsk, nw, batch, seq):
    t = zx.shape[0]
    nc = seq // CHUNK
    row = lambda b, c: (b * nc + c, 0)
    const = lambda b, c: (0, 0)
    return pl.pallas_call(
        _ssd_kernel,
        grid=(batch, nc),
        in_specs=[
            pl.BlockSpec((CHUNK, ZX_W), row),
            pl.BlockSpec((CHUNK, LANES), row),
            pl.BlockSpec((SUBLANES, SSD_XBC), const),
            pl.BlockSpec((1, SSD_XBC), const),
            pl.BlockSpec((1, LANES), const),
            pl.BlockSpec((1, LANES), const),
            pl.BlockSpec((1, SSD_WIDTH), const),
            pl.BlockSpec((1, SSD_WIDTH), const),
        ],
        out_specs=pl.BlockSpec((CHUNK, SSD_WIDTH), row),
        out_shape=jax.ShapeDtypeStruct((t, SSD_WIDTH), BF16),
        scratch_shapes=[
            pltpu.VMEM((CHUNK + SUBLANES, SSD_XBC), F32),
            pltpu.VMEM((SSD_HEADS, SSD_STATE, SSD_HEADDIM), F32),
            pltpu.VMEM((CHUNK, SSD_WIDTH), F32),
        ],
        compiler_params=_params("parallel", "arbitrary"),
    )(zx, dtp, conv_w8, conv_b, dtb, alog, dsk, nw)


def _attn_kernel(q_ref, k_ref, vt_ref, lam_ref, nw_ref, o_ref, acc_ref, *, lambda_init):
    tq = ATTN_TILE
    qi = pl.program_id(2)
    q = q_ref[...]
    lane = lax.broadcasted_iota(jnp.int32, q.shape, 1)
    zero = jnp.zeros_like(q)
    qmaps = (jnp.where(lane < DIFF_HEAD_DIM, q, zero), jnp.where(lane >= DIFF_HEAD_DIM, q, zero))
    acc_ref[...] = jnp.zeros_like(acc_ref)

    def tile(kt, vt, carry, masked):
        out = []
        for m in range(2):
            s = lax.dot_general(kt, qmaps[m], (((1,), (1,)), ((), ())),
                                preferred_element_type=F32)
            if masked:
                krow = lax.broadcasted_iota(jnp.int32, s.shape, 0)
                qcol = lax.broadcasted_iota(jnp.int32, s.shape, 1)
                s = jnp.where(krow <= qcol, s, -jnp.inf)
            m_old, l_old = carry[m]
            m_new = jnp.maximum(m_old, jnp.max(s, axis=0, keepdims=True))
            p = jnp.exp(s - m_new)
            alpha = jnp.exp(m_old - m_new)
            l_new = alpha * l_old + jnp.sum(p, axis=0, keepdims=True)
            acc_ref[m] = alpha * acc_ref[m] + jnp.dot(vt, p.astype(BF16), preferred_element_type=F32)
            out.append((m_new, l_new))
        return tuple(out)

    def body(j, carry):
        start = pl.multiple_of(j * tq, tq)
        return tile(k_ref[pl.ds(start, tq), :], vt_ref[:, pl.ds(start, tq)], carry, False)

    init = tuple((jnp.full((1, tq), -jnp.inf, F32), jnp.zeros((1, tq), F32)) for _ in range(2))
    carry = lax.fori_loop(0, qi, body, init)
    start = pl.multiple_of(qi * tq, tq)
    carry = tile(k_ref[pl.ds(start, tq), :], vt_ref[:, pl.ds(start, tq)], carry, True)

    lq = lam_ref[...]
    lam = (jnp.exp(jnp.sum(lq[0:1] * lq[1:2], axis=-1, keepdims=True))
           - jnp.exp(jnp.sum(lq[2:3] * lq[3:4], axis=-1, keepdims=True)) + lambda_init)
    o = acc_ref[0] / carry[0][1] - lam * (acc_ref[1] / carry[1][1])
    ms = jnp.mean(o * o, axis=0, keepdims=True)
    o = o * lax.rsqrt(ms + EPS) * nw_ref[...] * (1.0 - lambda_init)
    o_ref[...] = o.T.astype(BF16)


def _attention(q, k, vt, lam_qk, nw_col, lambda_init, batch, seq):
    t = q.shape[0]
    tq = ATTN_TILE
    nq = seq // tq
    return pl.pallas_call(
        functools.partial(_attn_kernel, lambda_init=lambda_init),
        grid=(batch, DIFF_HEADS, nq),
        in_specs=[
            pl.BlockSpec((tq, LANES), lambda b, h, i: (b * nq + i, h)),
            pl.BlockSpec((seq, LANES), lambda b, h, i: (b, h)),
            pl.BlockSpec((LANES, seq), lambda b, h, i: (h, b)),
            pl.BlockSpec((4, DIFF_HEAD_DIM), lambda b, h, i: (0, 0)),
            pl.BlockSpec((DIFF_V_DIM, 1), lambda b, h, i: (0, 0)),
        ],
        out_specs=pl.BlockSpec((tq, LANES), lambda b, h, i: (b * nq + i, h)),
        out_shape=jax.ShapeDtypeStruct((t, ATTN_V), BF16),
        scratch_shapes=[pltpu.VMEM((2, DIFF_V_DIM, tq), F32)],
        compiler_params=_params("parallel", "parallel", "arbitrary"),
    )(q, k, vt, lam_qk, nw_col)


def _layer_norm(r, g, b):
    mu = jnp.mean(r, axis=-1, keepdims=True)
    d = r - mu
    var = jnp.mean(d * d, axis=-1, keepdims=True)
    return d * lax.rsqrt(var + EPS) * g + b


def _outproj_kernel(ys_ref, ya_ref, x_ref, gate_ref, w_ref, g_ref, b_ref, o_ref, *, alpha):
    y = (jnp.dot(ys_ref[...], w_ref[0:SSD_WIDTH, :], preferred_element_type=F32)
         + jnp.dot(ya_ref[...], w_ref[SSD_WIDTH:D_MODEL, :], preferred_element_type=F32))
    r = alpha * x_ref[...] + (1.0 + gate_ref[0]) * y
    o_ref[...] = _layer_norm(r, g_ref[...], b_ref[...])


def _outproj(ys, ya, x2, mod3, w_out, g, b, alpha, seq):
    t, d = x2.shape
    tm = ROW_TILE
    per_seq = seq // tm
    row = lambda i: (i, 0)
    const = lambda i: (0, 0)
    return pl.pallas_call(
        functools.partial(_outproj_kernel, alpha=alpha),
        grid=(t // tm,),
        in_specs=[
            pl.BlockSpec((tm, SSD_WIDTH), row),
            pl.BlockSpec((tm, ATTN_V), row),
            pl.BlockSpec((tm, d), row),
            pl.BlockSpec((1, 1, d), lambda i: ((i // per_seq) * N_MOD + 2, 0, 0)),
            pl.BlockSpec((d, d), const),
            pl.BlockSpec((1, d), const),
            pl.BlockSpec((1, d), const),
        ],
        out_specs=pl.BlockSpec((tm, d), row),
        out_shape=jax.ShapeDtypeStruct((t, d), F32),
        compiler_params=_params("parallel"),
    )(ys, ya, x2, mod3, w_out, g, b)


def _ffn_kernel(x_ref, sc_ref, sh_ref, gate_ref, wg_ref, wu_ref, wd_ref, g_ref, b_ref,
                o_ref, h_ref, acc_ref, *, alpha):
    j = pl.program_id(1)

    @pl.when(j == 0)
    def _():
        h_ref[...] = (x_ref[...] * (1.0 + sc_ref[0]) + sh_ref[0]).astype(BF16)
        acc_ref[...] = jnp.zeros_like(acc_ref)

    h = h_ref[...]
    gt = jnp.dot(h, wg_ref[...], preferred_element_type=F32)
    up = jnp.dot(h, wu_ref[...], preferred_element_type=F32)
    act = (gt * _sigmoid(gt) * up).astype(BF16)
    acc_ref[...] += jnp.dot(act, wd_ref[...], preferred_element_type=F32)

    @pl.when(j == pl.num_programs(1) - 1)
    def _():
        r = alpha * x_ref[...] + (1.0 + gate_ref[0]) * acc_ref[...]
        o_ref[...] = _layer_norm(r, g_ref[...], b_ref[...])


def _ffn(x2, mod3, w_gu, w_down, g, b, alpha, seq):
    t, d = x2.shape
    tm = ROW_TILE
    tf = FF_TILE
    nf = D_FF // tf
    per_seq = seq // tm
    row = lambda i, j: (i, 0)
    const = lambda i, j: (0, 0)
    return pl.pallas_call(
        functools.partial(_ffn_kernel, alpha=alpha),
        grid=(t // tm, nf),
        in_specs=[
            pl.BlockSpec((tm, d), row),
            pl.BlockSpec((1, 1, d), lambda i, j: ((i // per_seq) * N_MOD + 4, 0, 0)),
            pl.BlockSpec((1, 1, d), lambda i, j: ((i // per_seq) * N_MOD + 3, 0, 0)),
            pl.BlockSpec((1, 1, d), lambda i, j: ((i // per_seq) * N_MOD + 5, 0, 0)),
            pl.BlockSpec((d, tf), lambda i, j: (0, j)),
            pl.BlockSpec((d, tf), lambda i, j: (0, nf + j)),
            pl.BlockSpec((tf, d), lambda i, j: (j, 0)),
            pl.BlockSpec((1, d), const),
            pl.BlockSpec((1, d), const),
        ],
        out_specs=pl.BlockSpec((tm, d), row),
        out_shape=jax.ShapeDtypeStruct((t, d), F32),
        scratch_shapes=[pltpu.VMEM((tm, d), BF16), pltpu.VMEM((tm, d), F32)],
        compiler_params=_params("parallel", "arbitrary"),
    )(x2, mod3, mod3, mod3, w_gu, w_gu, w_down, g, b)


def _rope_tables(seq):
    dim = DIFF_HEAD_DIM
    inv = 1.0 / (ROPE_THETA ** (jnp.arange(0, dim, 2, dtype=F32) / dim))
    ang = jnp.arange(seq, dtype=F32)[:, None] * inv[None, :]
    ang = jnp.concatenate([ang, ang, ang, ang], -1)
    sign = jnp.where((jnp.arange(LANES) % dim) < dim // 2, -1.0, 1.0).astype(F32)
    return jnp.cos(ang), jnp.sin(ang) * sign


def _pack_w_in(w):
    z_xbc = w[:, 0:ZX_W]
    dt = w[:, ZX_W:ZX_W + SSD_HEADS]
    qkv = w[:, ZX_W + SSD_HEADS:]
    pad = jnp.zeros((w.shape[0], LANES - SSD_HEADS), w.dtype)
    return jnp.concatenate([z_xbc, qkv, dt, pad], axis=1).astype(BF16)


def _pad_lanes(v, width):
    return jnp.pad(v, (0, width - v.shape[0])).reshape(1, width)


def kernel(x, c, w_mod, b_mod, w_in, conv_w, conv_b, dt_bias, a_log, d_skip, ssd_norm_w, lam_qk,
           attn_norm_w, w_out, ln1_g, ln1_b, w_gate_up, w_down, ln2_g, ln2_b):
    batch, seq, d = x.shape
    depth = w_mod.shape[0]
    assert d == D_MODEL and seq % ROW_TILE == 0 and seq % ATTN_TILE == 0 and batch <= SUBLANES
    t = batch * seq
    alpha = (2 * depth) ** 0.25

    cos2, sin2 = _rope_tables(seq)
    c_pad = jnp.pad(c, ((0, SUBLANES - batch), (0, 0)))
    mod = _modulation(c_pad, w_mod, b_mod)

    x2 = x.reshape(t, d)
    for l in range(depth):
        lambda_init = 0.8 - 0.6 * math.exp(-0.3 * l)
        mod3 = mod[l, :batch].reshape(batch * N_MOD, 1, d)
        zx, dtp, q, k, v = _inproj(x2, mod3, _pack_w_in(w_in[l]), cos2, sin2, seq)
        y_ssd = _ssd(zx, dtp,
                     jnp.pad(conv_w[l], ((0, SUBLANES - CONV_WIDTH), (0, 0))),
                     conv_b[l].reshape(1, SSD_XBC),
                     _pad_lanes(dt_bias[l], LANES), _pad_lanes(a_log[l], LANES),
                     jnp.repeat(d_skip[l], SSD_HEADDIM).reshape(1, SSD_WIDTH),
                     ssd_norm_w[l].reshape(1, SSD_WIDTH), batch, seq)
        y_attn = _attention(q, k, v.T, lam_qk[l], attn_norm_w[l].reshape(DIFF_V_DIM, 1),
                            lambda_init, batch, seq)
        x2 = _outproj(y_ssd, y_attn, x2, mod3, w_out[l].astype(BF16),
                      ln1_g[l].reshape(1, d), ln1_b[l].reshape(1, d), alpha, seq)
        x2 = _ffn(x2, mod3, w_gate_up[l].astype(BF16), w_down[l].astype(BF16),
                  ln2_g[l].reshape(1, d), ln2_b[l].reshape(1, d), alpha, seq)
    return x2.reshape(batch, seq, d)
```

```python
import functools
import math

import jax
import jax.numpy as jnp
from jax import lax
from jax.experimental import pallas as pl
from jax.experimental.pallas import tpu as pltpu

F32 = jnp.float32
BF16 = jnp.bfloat16

D_MODEL = 1024
SSD_WIDTH = 512
SSD_HEADDIM = 64
SSD_HEADS = 8
SSD_GROUPS = 2
SSD_HEADS_PER_GROUP = 4
SSD_STATE = 128
SSD_XBC = SSD_WIDTH + 2 * SSD_GROUPS * SSD_STATE
CONV_WIDTH = 4
CHUNK = 128
DIFF_HEAD_DIM = 64
DIFF_HEADS = 4
DIFF_V_DIM = 128
ATTN_QK = 512
ATTN_V = 512
ROPE_THETA = 10000.0
D_FF = 2816
N_MOD = 6
EPS = 1e-5

LANES = 128
SUBLANES = 8
VMEM_LIMIT_BYTES = 56 * 1024 * 1024

ZX_W = SSD_WIDTH + SSD_XBC
Q_OFF = ZX_W
K_OFF = Q_OFF + ATTN_QK
V_OFF = K_OFF + ATTN_QK
DT_OFF = V_OFF + ATTN_V
IN_PACKED = DT_OFF + LANES

BF16_SUBLANES = 16
VT_ROWS = DIFF_V_DIM + BF16_SUBLANES
LOG2E = math.log2(math.e)

ROW_TILE = 512
ATTN_TILE = 512
FF_TILE = 1408


def _sigmoid(x):
    return 1.0 / (1.0 + jnp.exp(-x))


def _params(*sem):
    return pltpu.CompilerParams(dimension_semantics=sem, vmem_limit_bytes=VMEM_LIMIT_BYTES)


def _mod_kernel(c_ref, w_ref, b_ref, o_ref):
    c = c_ref[...]
    cond = (c * _sigmoid(c)).astype(BF16)
    o_ref[0] = jnp.dot(cond, w_ref[0].astype(BF16), preferred_element_type=F32) + b_ref[0]


def _modulation(c_pad, w_mod, b_mod):
    depth, d, n = w_mod.shape
    tn = 2048
    return pl.pallas_call(
        _mod_kernel,
        grid=(depth, n // tn),
        in_specs=[
            pl.BlockSpec((SUBLANES, d), lambda l, j: (0, 0)),
            pl.BlockSpec((1, d, tn), lambda l, j: (l, 0, j)),
            pl.BlockSpec((1, 1, tn), lambda l, j: (l, 0, j)),
        ],
        out_specs=pl.BlockSpec((1, SUBLANES, tn), lambda l, j: (l, 0, j)),
        out_shape=jax.ShapeDtypeStruct((depth, SUBLANES, n), F32),
        compiler_params=_params("parallel", "parallel"),
    )(c_pad, w_mod, b_mod.reshape(depth, 1, n))


def _inproj_kernel(x_ref, sc_ref, sh_ref, w_ref, cos_ref, sin_ref,
                   zx_ref, dt_ref, q_ref, k_ref, v_ref):
    h = (x_ref[...] * (1.0 + sc_ref[0]) + sh_ref[0]).astype(BF16)
    zx_ref[...] = jnp.dot(h, w_ref[:, 0:ZX_W], preferred_element_type=F32)
    cos = cos_ref[...]
    sin = sin_ref[...]
    lane = lax.broadcasted_iota(jnp.int32, cos.shape, 1)
    first_half = (lane % DIFF_HEAD_DIM) < (DIFF_HEAD_DIM // 2)

    def rope(t):
        rot = jnp.where(first_half, pltpu.roll(t, LANES - DIFF_HEAD_DIM // 2, 1),
                        pltpu.roll(t, DIFF_HEAD_DIM // 2, 1))
        return t * cos + rot * sin

    for hd in range(DIFF_HEADS):
        lo = hd * LANES
        qh = jnp.dot(h, w_ref[:, Q_OFF + lo:Q_OFF + lo + LANES], preferred_element_type=F32)
        q_ref[:, lo:lo + LANES] = (rope(qh) * (LOG2E * DIFF_HEAD_DIM ** -0.5)).astype(BF16)
        kh = jnp.dot(h, w_ref[:, K_OFF + lo:K_OFF + lo + LANES], preferred_element_type=F32)
        k_ref[:, lo:lo + LANES] = rope(kh).astype(BF16)
    v_ref[...] = jnp.dot(h, w_ref[:, V_OFF:V_OFF + ATTN_V], preferred_element_type=F32).astype(BF16)
    dt_ref[...] = jnp.dot(h, w_ref[:, DT_OFF:DT_OFF + LANES], preferred_element_type=F32)


def _inproj(x2, mod3, w_packed, cos2, sin2, seq):
    t, d = x2.shape
    tm = ROW_TILE
    per_seq = seq // tm
    row = lambda i: (i, 0)
    return pl.pallas_call(
        _inproj_kernel,
        grid=(t // tm,),
        in_specs=[
            pl.BlockSpec((tm, d), row),
            pl.BlockSpec((1, 1, d), lambda i: ((i // per_seq) * N_MOD + 1, 0, 0)),
            pl.BlockSpec((1, 1, d), lambda i: ((i // per_seq) * N_MOD + 0, 0, 0)),
            pl.BlockSpec((d, IN_PACKED), lambda i: (0, 0)),
            pl.BlockSpec((tm, LANES), lambda i: (i % per_seq, 0)),
            pl.BlockSpec((tm, LANES), lambda i: (i % per_seq, 0)),
        ],
        out_specs=[
            pl.BlockSpec((tm, ZX_W), row),
            pl.BlockSpec((tm, LANES), row),
            pl.BlockSpec((tm, ATTN_QK), row),
            pl.BlockSpec((tm, ATTN_QK), row),
            pl.BlockSpec((tm, ATTN_V), row),
        ],
        out_shape=[
            jax.ShapeDtypeStruct((t, ZX_W), F32),
            jax.ShapeDtypeStruct((t, LANES), F32),
            jax.ShapeDtypeStruct((t, ATTN_QK), BF16),
            jax.ShapeDtypeStruct((t, ATTN_QK), BF16),
            jax.ShapeDtypeStruct((t, ATTN_V), BF16),
        ],
        compiler_params=_params("parallel"),
    )(x2, mod3, mod3, w_packed, cos2, sin2)


def _ssd_kernel(zx_ref, dt_ref, cw_ref, cb_ref, dtb_ref, alog_ref, dsk_ref, nw_ref,
                y_ref, xpad, state, ybuf):
    L = CHUNK
    c = pl.program_id(1)

    @pl.when(c == 0)
    def _():
        xpad[...] = jnp.zeros_like(xpad)
        state[...] = jnp.zeros_like(state)

    xpad[0:SUBLANES, :] = xpad[L:L + SUBLANES, :]
    xpad[SUBLANES:SUBLANES + L, :] = zx_ref[:, SSD_WIDTH:ZX_W]
    conv = cb_ref[...]
    for kk in range(CONV_WIDTH):
        off = SUBLANES - (CONV_WIDTH - 1) + kk
        conv = conv + cw_ref[kk:kk + 1, :] * xpad[off:off + L, :]
    xc = conv * _sigmoid(conv)
    xs = xc[:, 0:SSD_WIDTH]

    pre = dt_ref[...] + dtb_ref[...]
    dt = jnp.maximum(pre, 0.0) + jnp.log(1.0 + jnp.exp(-jnp.abs(pre)))
    a = dt * (-jnp.exp(alog_ref[...]))
    row = lax.broadcasted_iota(jnp.int32, (L, L), 0)
    col = lax.broadcasted_iota(jnp.int32, (L, L), 1)
    causal = row >= col
    cs = jnp.dot(causal.astype(F32), a, precision=lax.Precision.HIGHEST,
                 preferred_element_type=F32)
    cs_t = cs.T

    for g in range(SSD_GROUPS):
        bm = xc[:, SSD_WIDTH + g * SSD_STATE:SSD_WIDTH + (g + 1) * SSD_STATE]
        cm = xc[:, SSD_WIDTH + (SSD_GROUPS + g) * SSD_STATE:SSD_WIDTH + (SSD_GROUPS + g + 1) * SSD_STATE]
        cm_b = cm.astype(BF16)
        cb = lax.dot_general(cm_b, bm.astype(BF16), (((1,), (1,)), ((), ())),
                             preferred_element_type=F32)
        bm_t = bm.T
        for e in range(SSD_HEADS_PER_GROUP):
            hd = g * SSD_HEADS_PER_GROUP + e
            cs_col = cs[:, hd:hd + 1]
            cs_row = cs_t[hd:hd + 1, :]
            cs_last = cs_t[hd:hd + 1, L - 1:L]
            lmat = jnp.exp(jnp.where(causal, cs_col - cs_row, -jnp.inf))
            xdt = (xs[:, hd * SSD_HEADDIM:(hd + 1) * SSD_HEADDIM] * dt[:, hd:hd + 1]).astype(BF16)
            y_diag = jnp.dot((cb * lmat).astype(BF16), xdt, preferred_element_type=F32)
            prev = state[hd]
            y_off = jnp.dot(cm_b, prev.astype(BF16), preferred_element_type=F32) * jnp.exp(cs_col)
            decay = jnp.exp(cs_last - cs_row)
            st = jnp.dot((bm_t * decay).astype(BF16), xdt, preferred_element_type=F32)
            state[hd] = jnp.exp(cs_last) * prev + st
            ybuf[:, hd * SSD_HEADDIM:(hd + 1) * SSD_HEADDIM] = y_diag + y_off

    z = zx_ref[:, 0:SSD_WIDTH]
    y = (ybuf[...] + dsk_ref[...] * xs) * (z * _sigmoid(z))
    gw = SSD_WIDTH // SSD_GROUPS
    for g in range(SSD_GROUPS):
        yg = y[:, g * gw:(g + 1) * gw]
        ms = jnp.mean(yg * yg, axis=-1, keepdims=True)
        y_ref[:, g * gw:(g + 1) * gw] = (yg * lax.rsqrt(ms + EPS)
                                         * nw_ref[:, g * gw:(g + 1) * gw]).astype(BF16)


def _ssd(zx, dtp, conv_w8, conv_b, dtb, alog, dsk, nw, batch, seq):
    t = zx.shape[0]
    nc = seq // CHUNK
    row = lambda b, c: (b * nc + c, 0)
    const = lambda b, c: (0, 0)
    return pl.pallas_call(
        _ssd_kernel,
        grid=(batch, nc),
        in_specs=[
            pl.BlockSpec((CHUNK, ZX_W), row),
            pl.BlockSpec((CHUNK, LANES), row),
            pl.BlockSpec((SUBLANES, SSD_XBC), const),
            pl.BlockSpec((1, SSD_XBC), const),
            pl.BlockSpec((1, LANES), const),
            pl.BlockSpec((1, LANES), const),
            pl.BlockSpec((1, SSD_WIDTH), const),
            pl.BlockSpec((1, SSD_WIDTH), const),
        ],
        out_specs=pl.BlockSpec((CHUNK, SSD_WIDTH), row),
        out_shape=jax.ShapeDtypeStruct((t, SSD_WIDTH), BF16),
        scratch_shapes=[
            pltpu.VMEM((CHUNK + SUBLANES, SSD_XBC), F32),
            pltpu.VMEM((SSD_HEADS, SSD_STATE, SSD_HEADDIM), F32),
            pltpu.VMEM((CHUNK, SSD_WIDTH), F32),
        ],
        compiler_params=_params("parallel", "arbitrary"),
    )(zx, dtp, conv_w8, conv_b, dtb, alog, dsk, nw)


def _attn_kernel(q_ref, k_ref, vt_ref, lam_ref, nw_ref, o_ref, sa_ref, sb_ref, acc_ref, m_ref,
                 *, lambda_init):
    tq = ATTN_TILE
    qi = pl.program_id(2)
    q = q_ref[...]
    lane = lax.broadcasted_iota(jnp.int32, q.shape, 1)
    zero = jnp.zeros_like(q)
    qmaps = (jnp.where(lane < DIFF_HEAD_DIM, q, zero), jnp.where(lane >= DIFF_HEAD_DIM, q, zero))
    acc_ref[...] = jnp.zeros_like(acc_ref)

    m_ref[...] = jnp.full(m_ref.shape, -jnp.inf, F32)

    def scores(j, s_ref):
        kt = k_ref[pl.ds(pl.multiple_of(j * tq, tq), tq), :]
        for m in range(2):
            s_ref[m] = lax.dot_general(kt, qmaps[m], (((1,), (1,)), ((), ())),
                                       preferred_element_type=F32)

    def consume(j, s_ref, masked):
        vt = vt_ref[:, pl.ds(pl.multiple_of(j * tq, tq), tq)]
        for m in range(2):
            s = s_ref[m]
            if masked:
                krow = lax.broadcasted_iota(jnp.int32, s.shape, 0)
                qcol = lax.broadcasted_iota(jnp.int32, s.shape, 1)
                s = jnp.where(krow <= qcol, s, -jnp.inf)
            m_old = m_ref[m]
            m_new = jnp.maximum(m_old, jnp.max(s, axis=0, keepdims=True))
            p = jnp.exp2(s - m_new).astype(BF16)
            acc_ref[m] = jnp.exp2(m_old - m_new) * acc_ref[m] + jnp.dot(vt, p, preferred_element_type=F32)
            m_ref[m] = m_new

    scores(0, sa_ref)

    @pl.loop(0, qi // 2)
    def _(jj):
        j = 2 * jj
        scores(j + 1, sb_ref)
        consume(j, sa_ref, False)
        scores(j + 2, sa_ref)
        consume(j + 1, sb_ref, False)

    @pl.when(qi % 2 == 0)
    def _():
        consume(qi, sa_ref, True)

    @pl.when(qi % 2 == 1)
    def _():
        scores(qi, sb_ref)
        consume(qi - 1, sa_ref, False)
        consume(qi, sb_ref, True)

    lq = lam_ref[...]
    lam = (jnp.exp(jnp.sum(lq[0:1] * lq[1:2], axis=-1, keepdims=True))
           - jnp.exp(jnp.sum(lq[2:3] * lq[3:4], axis=-1, keepdims=True)) + lambda_init)
    dv = DIFF_V_DIM
    o = (acc_ref[0, 0:dv, :] / acc_ref[0, dv:dv + 1, :]
         - lam * (acc_ref[1, 0:dv, :] / acc_ref[1, dv:dv + 1, :]))
    ms = jnp.mean(o * o, axis=0, keepdims=True)
    o = o * lax.rsqrt(ms + EPS) * nw_ref[...] * (1.0 - lambda_init)
    o_ref[...] = o.T.astype(BF16)


def _attention(q, k, vt, lam_qk, nw_col, lambda_init, batch, seq):
    t = q.shape[0]
    tq = ATTN_TILE
    nq = seq // tq
    return pl.pallas_call(
        functools.partial(_attn_kernel, lambda_init=lambda_init),
        grid=(batch, DIFF_HEADS, nq),
        in_specs=[
            pl.BlockSpec((tq, LANES), lambda b, h, i: (b * nq + i, h)),
            pl.BlockSpec((seq, LANES), lambda b, h, i: (b, h)),
            pl.BlockSpec((VT_ROWS, seq), lambda b, h, i: (h, b)),
            pl.BlockSpec((4, DIFF_HEAD_DIM), lambda b, h, i: (0, 0)),
            pl.BlockSpec((DIFF_V_DIM, 1), lambda b, h, i: (0, 0)),
        ],
        out_specs=pl.BlockSpec((tq, LANES), lambda b, h, i: (b * nq + i, h)),
        out_shape=jax.ShapeDtypeStruct((t, ATTN_V), BF16),
        scratch_shapes=[pltpu.VMEM((2, tq, tq), F32), pltpu.VMEM((2, tq, tq), F32),
                        pltpu.VMEM((2, VT_ROWS, tq), F32), pltpu.VMEM((2, 1, tq), F32)],
        compiler_params=_params("parallel", "parallel", "arbitrary"),
    )(q, k, vt, lam_qk, nw_col)


def _layer_norm(r, g, b):
    mu = jnp.mean(r, axis=-1, keepdims=True)
    d = r - mu
    var = jnp.mean(d * d, axis=-1, keepdims=True)
    return d * lax.rsqrt(var + EPS) * g + b


def _outproj_kernel(ys_ref, ya_ref, x_ref, gate_ref, w_ref, g_ref, b_ref, o_ref, *, alpha):
    y = (jnp.dot(ys_ref[...], w_ref[0:SSD_WIDTH, :], preferred_element_type=F32)
         + jnp.dot(ya_ref[...], w_ref[SSD_WIDTH:D_MODEL, :], preferred_element_type=F32))
    r = alpha * x_ref[...] + (1.0 + gate_ref[0]) * y
    o_ref[...] = _layer_norm(r, g_ref[...], b_ref[...])


def _outproj(ys, ya, x2, mod3, w_out, g, b, alpha, seq):
    t, d = x2.shape
    tm = ROW_TILE
    per_seq = seq // tm
    row = lambda i: (i, 0)
    const = lambda i: (0, 0)
    return pl.pallas_call(
        functools.partial(_outproj_kernel, alpha=alpha),
        grid=(t // tm,),
        in_specs=[
            pl.BlockSpec((tm, SSD_WIDTH), row),
            pl.BlockSpec((tm, ATTN_V), row),
            pl.BlockSpec((tm, d), row),
            pl.BlockSpec((1, 1, d), lambda i: ((i // per_seq) * N_MOD + 2, 0, 0)),
            pl.BlockSpec((d, d), const),
            pl.BlockSpec((1, d), const),
            pl.BlockSpec((1, d), const),
        ],
        out_specs=pl.BlockSpec((tm, d), row),
        out_shape=jax.ShapeDtypeStruct((t, d), F32),
        compiler_params=_params("parallel"),
    )(ys, ya, x2, mod3, w_out, g, b)


def _ffn_kernel(x_ref, sc_ref, sh_ref, gate_ref, wg_ref, wu_ref, wd_ref, g_ref, b_ref,
                o_ref, h_ref, acc_ref, *, alpha):
    j = pl.program_id(1)

    @pl.when(j == 0)
    def _():
        h_ref[...] = (x_ref[...] * (1.0 + sc_ref[0]) + sh_ref[0]).astype(BF16)
        acc_ref[...] = jnp.zeros_like(acc_ref)

    h = h_ref[...]
    gt = jnp.dot(h, wg_ref[...], preferred_element_type=F32)
    up = jnp.dot(h, wu_ref[...], preferred_element_type=F32)
    act = (gt * _sigmoid(gt) * up).astype(BF16)
    acc_ref[...] += jnp.dot(act, wd_ref[...], preferred_element_type=F32)

    @pl.when(j == pl.num_programs(1) - 1)
    def _():
        r = alpha * x_ref[...] + (1.0 + gate_ref[0]) * acc_ref[...]
        o_ref[...] = _layer_norm(r, g_ref[...], b_ref[...])


def _ffn(x2, mod3, w_gu, w_down, g, b, alpha, seq):
    t, d = x2.shape
    tm = ROW_TILE
    tf = FF_TILE
    nf = D_FF // tf
    per_seq = seq // tm
    row = lambda i, j: (i, 0)
    const = lambda i, j: (0, 0)
    return pl.pallas_call(
        functools.partial(_ffn_kernel, alpha=alpha),
        grid=(t // tm, nf),
        in_specs=[
            pl.BlockSpec((tm, d), row),
            pl.BlockSpec((1, 1, d), lambda i, j: ((i // per_seq) * N_MOD + 4, 0, 0)),
            pl.BlockSpec((1, 1, d), lambda i, j: ((i // per_seq) * N_MOD + 3, 0, 0)),
            pl.BlockSpec((1, 1, d), lambda i, j: ((i // per_seq) * N_MOD + 5, 0, 0)),
            pl.BlockSpec((d, tf), lambda i, j: (0, j)),
            pl.BlockSpec((d, tf), lambda i, j: (0, nf + j)),
            pl.BlockSpec((tf, d), lambda i, j: (j, 0)),
            pl.BlockSpec((1, d), const),
            pl.BlockSpec((1, d), const),
        ],
        out_specs=pl.BlockSpec((tm, d), row),
        out_shape=jax.ShapeDtypeStruct((t, d), F32),
        scratch_shapes=[pltpu.VMEM((tm, d), BF16), pltpu.VMEM((tm, d), F32)],
        compiler_params=_params("parallel", "arbitrary"),
    )(x2, mod3, mod3, mod3, w_gu, w_gu, w_down, g, b)


def _rope_tables(seq):
    dim = DIFF_HEAD_DIM
    inv = 1.0 / (ROPE_THETA ** (jnp.arange(0, dim, 2, dtype=F32) / dim))
    ang = jnp.arange(seq, dtype=F32)[:, None] * inv[None, :]
    ang = jnp.concatenate([ang, ang, ang, ang], -1)
    sign = jnp.where((jnp.arange(LANES) % dim) < dim // 2, -1.0, 1.0).astype(F32)
    return jnp.cos(ang), jnp.sin(ang) * sign


def _pack_w_in(w):
    z_xbc = w[:, 0:ZX_W]
    dt = w[:, ZX_W:ZX_W + SSD_HEADS]
    qkv = w[:, ZX_W + SSD_HEADS:]
    pad = jnp.zeros((w.shape[0], LANES - SSD_HEADS), w.dtype)
    return jnp.concatenate([z_xbc, qkv, dt, pad], axis=1).astype(BF16)


def _pad_lanes(v, width):
    return jnp.pad(v, (0, width - v.shape[0])).reshape(1, width)


def kernel(x, c, w_mod, b_mod, w_in, conv_w, conv_b, dt_bias, a_log, d_skip, ssd_norm_w, lam_qk,
           attn_norm_w, w_out, ln1_g, ln1_b, w_gate_up, w_down, ln2_g, ln2_b):
    batch, seq, d = x.shape
    depth = w_mod.shape[0]
    assert d == D_MODEL and seq % ROW_TILE == 0 and seq % ATTN_TILE == 0 and batch <= SUBLANES
    t = batch * seq
    alpha = (2 * depth) ** 0.25

    cos2, sin2 = _rope_tables(seq)
    c_pad = jnp.pad(c, ((0, SUBLANES - batch), (0, 0)))
    mod = _modulation(c_pad, w_mod, b_mod)

    x2 = x.reshape(t, d)
    for l in range(depth):
        lambda_init = 0.8 - 0.6 * math.exp(-0.3 * l)
        mod3 = mod[l, :batch].reshape(batch * N_MOD, 1, d)
        zx, dtp, q, k, v = _inproj(x2, mod3, _pack_w_in(w_in[l]), cos2, sin2, seq)
        y_ssd = _ssd(zx, dtp,
                     jnp.pad(conv_w[l], ((0, SUBLANES - CONV_WIDTH), (0, 0))),
                     conv_b[l].reshape(1, SSD_XBC),
                     _pad_lanes(dt_bias[l], LANES), _pad_lanes(a_log[l], LANES),
                     jnp.repeat(d_skip[l], SSD_HEADDIM).reshape(1, SSD_WIDTH),
                     ssd_norm_w[l].reshape(1, SSD_WIDTH), batch, seq)
        vt = jnp.concatenate([v.T.reshape(DIFF_HEADS, DIFF_V_DIM, t),
                              jnp.ones((DIFF_HEADS, BF16_SUBLANES, t), BF16)], axis=1)
        y_attn = _attention(q, k, vt.reshape(DIFF_HEADS * VT_ROWS, t), lam_qk[l],
                            attn_norm_w[l].reshape(DIFF_V_DIM, 1), lambda_init, batch, seq)
        x2 = _outproj(y_ssd, y_attn, x2, mod3, w_out[l].astype(BF16),
                      ln1_g[l].reshape(1, d), ln1_b[l].reshape(1, d), alpha, seq)
        x2 = _ffn(x2, mod3, w_gate_up[l].astype(BF16), w_down[l].astype(BF16),
                  ln2_g[l].reshape(1, d), ln2_b[l].reshape(1, d), alpha, seq)
    return x2.reshape(batch, seq, d)
```

```python
import functools
import math

import jax
import jax.numpy as jnp
from jax import lax
from jax.experimental import pallas as pl
from jax.experimental.pallas import tpu as pltpu

F32 = jnp.float32
BF16 = jnp.bfloat16

D_MODEL = 1024
SSD_WIDTH = 512
SSD_HEADDIM = 64
SSD_HEADS = 8
SSD_GROUPS = 2
SSD_HEADS_PER_GROUP = 4
SSD_STATE = 128
SSD_XBC = SSD_WIDTH + 2 * SSD_GROUPS * SSD_STATE
CONV_WIDTH = 4
CHUNK = 128
DIFF_HEAD_DIM = 64
DIFF_HEADS = 4
DIFF_V_DIM = 128
ATTN_QK = 512
ATTN_V = 512
ROPE_THETA = 10000.0
D_FF = 2816
N_MOD = 6
EPS = 1e-5

LANES = 128
SUBLANES = 8
VMEM_LIMIT_BYTES = 56 * 1024 * 1024

ZX_W = SSD_WIDTH + SSD_XBC
Q_OFF = ZX_W
K_OFF = Q_OFF + ATTN_QK
V_OFF = K_OFF + ATTN_QK
DT_OFF = V_OFF + ATTN_V
IN_PACKED = DT_OFF + LANES

BF16_SUBLANES = 16
VT_ROWS = DIFF_V_DIM + BF16_SUBLANES
LOG2E = math.log2(math.e)

ROW_TILE = 512
ATTN_TILE = 512
FF_TILE = 1408
SSD_CHUNKS_PER_STEP = 4


def _sigmoid(x):
    return 1.0 / (1.0 + jnp.exp(-x))


def _silu(x):
    hx = 0.5 * x
    return hx + hx * jnp.tanh(hx)


def _params(*sem):
    return pltpu.CompilerParams(dimension_semantics=sem, vmem_limit_bytes=VMEM_LIMIT_BYTES)


def _mod_kernel(c_ref, w_ref, b_ref, o_ref):
    c = c_ref[...]
    cond = (c * _sigmoid(c)).astype(BF16)
    o_ref[0] = jnp.dot(cond, w_ref[0].astype(BF16), preferred_element_type=F32) + b_ref[0]


def _modulation(c_pad, w_mod, b_mod):
    depth, d, n = w_mod.shape
    tn = 2048
    return pl.pallas_call(
        _mod_kernel,
        grid=(depth, n // tn),
        in_specs=[
            pl.BlockSpec((SUBLANES, d), lambda l, j: (0, 0)),
            pl.BlockSpec((1, d, tn), lambda l, j: (l, 0, j)),
            pl.BlockSpec((1, 1, tn), lambda l, j: (l, 0, j)),
        ],
        out_specs=pl.BlockSpec((1, SUBLANES, tn), lambda l, j: (l, 0, j)),
        out_shape=jax.ShapeDtypeStruct((depth, SUBLANES, n), F32),
        compiler_params=_params("parallel", "parallel"),
    )(c_pad, w_mod, b_mod.reshape(depth, 1, n))


def _inproj_kernel(x_ref, sc_ref, sh_ref, w_ref, cos_ref, sin_ref,
                   zx_ref, dt_ref, q_ref, k_ref, vt_ref):
    h = (x_ref[...] * (1.0 + sc_ref[0]) + sh_ref[0]).astype(BF16)
    zx_ref[...] = jnp.dot(h, w_ref[:, 0:ZX_W], preferred_element_type=F32)
    cos = cos_ref[...]
    sin = sin_ref[...]
    lane = lax.broadcasted_iota(jnp.int32, cos.shape, 1)
    first_half = (lane % DIFF_HEAD_DIM) < (DIFF_HEAD_DIM // 2)

    def rope(t):
        rot = jnp.where(first_half, pltpu.roll(t, LANES - DIFF_HEAD_DIM // 2, 1),
                        pltpu.roll(t, DIFF_HEAD_DIM // 2, 1))
        return t * cos + rot * sin

    q = jnp.dot(h, w_ref[:, Q_OFF:Q_OFF + ATTN_QK], preferred_element_type=F32)
    k = jnp.dot(h, w_ref[:, K_OFF:K_OFF + ATTN_QK], preferred_element_type=F32)
    vd = jnp.dot(h, w_ref[:, V_OFF:IN_PACKED], preferred_element_type=F32)
    dt_ref[...] = vd[:, ATTN_V:ATTN_V + LANES]
    ones = jnp.ones((BF16_SUBLANES, h.shape[0]), BF16)
    for hd in range(DIFF_HEADS):
        lo = hd * LANES
        q_ref[:, lo:lo + LANES] = (rope(q[:, lo:lo + LANES])
                                   * (LOG2E * DIFF_HEAD_DIM ** -0.5)).astype(BF16)
        k_ref[:, lo:lo + LANES] = rope(k[:, lo:lo + LANES]).astype(BF16)
        vt_ref[hd * VT_ROWS:hd * VT_ROWS + DIFF_V_DIM, :] = vd[:, lo:lo + LANES].T.astype(BF16)
        vt_ref[hd * VT_ROWS + DIFF_V_DIM:(hd + 1) * VT_ROWS, :] = ones


def _inproj(x2, mod3, w_packed, cos2, sin2, seq):
    t, d = x2.shape
    tm = ROW_TILE
    per_seq = seq // tm
    row = lambda i: (i, 0)
    return pl.pallas_call(
        _inproj_kernel,
        grid=(t // tm,),
        in_specs=[
            pl.BlockSpec((tm, d), row),
            pl.BlockSpec((1, 1, d), lambda i: ((i // per_seq) * N_MOD + 1, 0, 0)),
            pl.BlockSpec((1, 1, d), lambda i: ((i // per_seq) * N_MOD + 0, 0, 0)),
            pl.BlockSpec((d, IN_PACKED), lambda i: (0, 0)),
            pl.BlockSpec((tm, LANES), lambda i: (i % per_seq, 0)),
            pl.BlockSpec((tm, LANES), lambda i: (i % per_seq, 0)),
        ],
        out_specs=[
            pl.BlockSpec((tm, ZX_W), row),
            pl.BlockSpec((tm, LANES), row),
            pl.BlockSpec((tm, ATTN_QK), row),
            pl.BlockSpec((tm, ATTN_QK), row),
            pl.BlockSpec((DIFF_HEADS * VT_ROWS, tm), lambda i: (0, i)),
        ],
        out_shape=[
            jax.ShapeDtypeStruct((t, ZX_W), F32),
            jax.ShapeDtypeStruct((t, LANES), F32),
            jax.ShapeDtypeStruct((t, ATTN_QK), BF16),
            jax.ShapeDtypeStruct((t, ATTN_QK), BF16),
            jax.ShapeDtypeStruct((DIFF_HEADS * VT_ROWS, t), BF16),
        ],
        compiler_params=_params("parallel"),
    )(x2, mod3, mod3, w_packed, cos2, sin2)


def _ssd_kernel(zx_ref, dt_ref, cw_ref, cb_ref, dtb_ref, alog_ref, dsk_ref, nw_ref, ex_ref,
                y_ref, xpad, state):
    L = CHUNK

    @pl.when(pl.program_id(1) == 0)
    def _():
        xpad[...] = jnp.zeros_like(xpad)
        state[...] = jnp.zeros_like(state)

    pl.loop(0, SSD_CHUNKS_PER_STEP)(functools.partial(
        _ssd_chunk, zx_ref, dt_ref, cw_ref, cb_ref, dtb_ref, alog_ref, dsk_ref, nw_ref, ex_ref,
        y_ref, xpad, state))


def _split3(x):
    hi = x.astype(BF16)
    r1 = x - hi.astype(F32)
    mid = r1.astype(BF16)
    lo = (r1 - mid.astype(F32)).astype(BF16)
    return hi, mid, lo


def _ssd_chunk(zx_ref, dt_ref, cw_ref, cb_ref, dtb_ref, alog_ref, dsk_ref, nw_ref, ex_ref,
               y_ref, xpad, state, ci):
    L = CHUNK
    rows = pl.ds(pl.multiple_of(ci * L, L), L)
    dot = functools.partial(jnp.dot, preferred_element_type=F32)

    xpad[0:SUBLANES, :] = xpad[L:L + SUBLANES, :]
    xpad[SUBLANES:SUBLANES + L, :] = zx_ref[rows, SSD_WIDTH:ZX_W]
    conv = cb_ref[...]
    for kk in range(CONV_WIDTH):
        off = SUBLANES - (CONV_WIDTH - 1) + kk
        conv = conv + cw_ref[kk:kk + 1, :] * xpad[off:off + L, :]
    xc = _silu(conv)
    xs = xc[:, 0:SSD_WIDTH]

    pre = dt_ref[rows, :] + dtb_ref[...]
    dt = jnp.maximum(pre, 0.0) + jnp.log(1.0 + jnp.exp(-jnp.abs(pre)))
    a = dt * (-jnp.exp(alog_ref[...]))
    row = lax.broadcasted_iota(jnp.int32, (L, L), 0)
    col = lax.broadcasted_iota(jnp.int32, (L, L), 1)
    causal = row >= col
    tril = causal.astype(BF16)
    cs = sum(dot(tril, p) for p in _split3(a))
    cs_t = cs.T
    dt_p = _split3(dt)
    xdt_even = (xs * sum(dot(p, ex_ref[0]) for p in dt_p)).astype(BF16)
    xdt_odd = (xs * sum(dot(p, ex_ref[1]) for p in dt_p)).astype(BF16)
    ecs_x = jnp.exp(sum(dot(p, ex_ref[2]) for p in _split3(cs)))

    ys = []
    for g in range(SSD_GROUPS):
        bm = xc[:, SSD_WIDTH + g * SSD_STATE:SSD_WIDTH + (g + 1) * SSD_STATE]
        cm = xc[:, SSD_WIDTH + (SSD_GROUPS + g) * SSD_STATE:SSD_WIDTH + (SSD_GROUPS + g + 1) * SSD_STATE]
        cm_b = cm.astype(BF16)
        cb = lax.dot_general(cm_b, bm.astype(BF16), (((1,), (1,)), ((), ())),
                             preferred_element_type=F32)
        bm_t = bm.T
        for pp in range(SSD_HEADS_PER_GROUP // 2):
            pr = g * (SSD_HEADS_PER_GROUP // 2) + pp
            halves = (xdt_even[:, pr * LANES:(pr + 1) * LANES],
                      xdt_odd[:, pr * LANES:(pr + 1) * LANES])
            y_pair = None
            st_pair = None
            for half in range(2):
                hd = 2 * pr + half
                cs_col = cs[:, hd:hd + 1]
                cs_row = cs_t[hd:hd + 1, :]
                cs_last = cs_t[hd:hd + 1, L - 1:L]
                lmat = jnp.exp(jnp.where(causal, cs_col - cs_row, -jnp.inf))
                y_h = jnp.dot((cb * lmat).astype(BF16), halves[half], preferred_element_type=F32)
                decay = jnp.exp(cs_last - cs_row)
                st_h = jnp.dot((bm_t * decay).astype(BF16), halves[half],
                               preferred_element_type=F32)
                y_pair = y_h if y_pair is None else y_pair + y_h
                st_pair = st_h if st_pair is None else st_pair + st_h
            prev = state[pr]
            ecs_p = ecs_x[:, pr * LANES:(pr + 1) * LANES]
            y_off = jnp.dot(cm_b, prev.astype(BF16), preferred_element_type=F32) * ecs_p
            state[pr] = ecs_p[L - 1:L, :] * prev + st_pair
            ys.append(y_pair + y_off)

    z = zx_ref[rows, 0:SSD_WIDTH]
    y = (jnp.concatenate(ys, axis=1) + dsk_ref[...] * xs) * _silu(z)
    gw = SSD_WIDTH // SSD_GROUPS
    for g in range(SSD_GROUPS):
        yg = y[:, g * gw:(g + 1) * gw]
        ms = jnp.mean(yg * yg, axis=-1, keepdims=True)
        y_ref[rows, g * gw:(g + 1) * gw] = (yg * lax.rsqrt(ms + EPS)
                                            * nw_ref[:, g * gw:(g + 1) * gw]).astype(BF16)


def _ssd(zx, dtp, conv_w8, conv_b, dtb, alog, dsk, nw, batch, seq):
    t = zx.shape[0]
    rows = CHUNK * SSD_CHUNKS_PER_STEP
    nc = seq // rows
    row = lambda b, c: (b * nc + c, 0)
    const = lambda b, c: (0, 0)
    const3 = lambda b, c: (0, 0, 0)
    head_of_col = jnp.arange(SSD_WIDTH)[None, :] // SSD_HEADDIM
    head = jnp.arange(LANES)[:, None]
    every = head_of_col == head
    expand = jnp.stack([every & (head % 2 == 0), every & (head % 2 == 1), every]).astype(BF16)
    return pl.pallas_call(
        _ssd_kernel,
        grid=(batch, nc),
        in_specs=[
            pl.BlockSpec((rows, ZX_W), row),
            pl.BlockSpec((rows, LANES), row),
            pl.BlockSpec((SUBLANES, SSD_XBC), const),
            pl.BlockSpec((1, SSD_XBC), const),
            pl.BlockSpec((1, LANES), const),
            pl.BlockSpec((1, LANES), const),
            pl.BlockSpec((1, SSD_WIDTH), const),
            pl.BlockSpec((1, SSD_WIDTH), const),
            pl.BlockSpec((3, LANES, SSD_WIDTH), const3),
        ],
        out_specs=pl.BlockSpec((rows, SSD_WIDTH), row),
        out_shape=jax.ShapeDtypeStruct((t, SSD_WIDTH), BF16),
        scratch_shapes=[
            pltpu.VMEM((CHUNK + SUBLANES, SSD_XBC), F32),
            pltpu.VMEM((SSD_HEADS // 2, SSD_STATE, 2 * SSD_HEADDIM), F32),
        ],
        compiler_params=_params("parallel", "arbitrary"),
    )(zx, dtp, conv_w8, conv_b, dtb, alog, dsk, nw, expand)


def _attn_kernel(q_ref, k_ref, vt_ref, lam_ref, nw_ref, o_ref, sa_ref, sb_ref, mxa_ref, mxb_ref,
                 acc_ref, m_ref, *, lambda_init, nq):
    tq = ATTN_TILE
    dv = DIFF_V_DIM
    buf_a = (sa_ref, mxa_ref)
    buf_b = (sb_ref, mxb_ref)
    lq = lam_ref[...]
    lam = (jnp.exp(jnp.sum(lq[0:1] * lq[1:2], axis=-1, keepdims=True))
           - jnp.exp(jnp.sum(lq[2:3] * lq[3:4], axis=-1, keepdims=True)) + lambda_init)

    def finalize(blk):
        r0 = 1.0 / acc_ref[0, dv:dv + 1, :]
        r1 = lam / acc_ref[1, dv:dv + 1, :]
        o = acc_ref[0, 0:dv, :] * r0 - acc_ref[1, 0:dv, :] * r1
        ms = jnp.mean(o * o, axis=0, keepdims=True)
        o = o * lax.rsqrt(ms + EPS) * nw_ref[...] * (1.0 - lambda_init)
        o_ref[pl.ds(pl.multiple_of(blk * tq, tq), tq), :] = o.T.astype(BF16)

    def scores(j, buf, qmaps):
        s_ref, mx_ref = buf
        kt = k_ref[pl.ds(pl.multiple_of(j * tq, tq), tq), :]
        for m in range(2):
            s = lax.dot_general(kt, qmaps[m], (((1,), (1,)), ((), ())),
                                preferred_element_type=F32)
            s_ref[m] = s
            mx_ref[m] = jnp.max(s, axis=0, keepdims=True)

    def consume(j, buf, masked):
        s_ref, mx_ref = buf
        vt = vt_ref[:, pl.ds(pl.multiple_of(j * tq, tq), tq)]
        for m in range(2):
            s = s_ref[m]
            m_old = m_ref[m]
            if masked:
                krow = lax.broadcasted_iota(jnp.int32, s.shape, 0)
                qcol = lax.broadcasted_iota(jnp.int32, s.shape, 1)
                s = jnp.where(krow <= qcol, s, -jnp.inf)
                m_new = jnp.maximum(m_old, jnp.max(s, axis=0, keepdims=True))
            else:
                m_new = jnp.maximum(m_old, mx_ref[m])
            p = jnp.exp2(s - m_new).astype(BF16)
            acc_ref[m] = jnp.exp2(m_old - m_new) * acc_ref[m] + jnp.dot(vt, p, preferred_element_type=F32)
            m_ref[m] = m_new

    acc_ref[...] = jnp.ones_like(acc_ref)

    @pl.loop(0, nq)
    def _(qi):
        q = q_ref[pl.ds(pl.multiple_of(qi * tq, tq), tq), :]
        lane = lax.broadcasted_iota(jnp.int32, q.shape, 1)
        zero = jnp.zeros_like(q)
        qmaps = (jnp.where(lane < DIFF_HEAD_DIM, q, zero), jnp.where(lane >= DIFF_HEAD_DIM, q, zero))

        finalize(jnp.maximum(qi - 1, 0))
        scores(0, buf_a, qmaps)
        acc_ref[...] = jnp.zeros_like(acc_ref)
        m_ref[...] = jnp.full(m_ref.shape, -jnp.inf, F32)

        @pl.loop(0, qi // 2)
        def _(jj):
            j = 2 * jj
            scores(j + 1, buf_b, qmaps)
            consume(j, buf_a, False)
            scores(j + 2, buf_a, qmaps)
            consume(j + 1, buf_b, False)

        @pl.when(qi % 2 == 0)
        def _():
            consume(qi, buf_a, True)

        @pl.when(qi % 2 == 1)
        def _():
            scores(qi, buf_b, qmaps)
            consume(qi - 1, buf_a, False)
            consume(qi, buf_b, True)

    finalize(nq - 1)


def _attention(q, k, vt, lam_qk, nw_col, lambda_init, batch, seq):
    t = q.shape[0]
    tq = ATTN_TILE
    nq = seq // tq
    return pl.pallas_call(
        functools.partial(_attn_kernel, lambda_init=lambda_init, nq=nq),
        grid=(batch, DIFF_HEADS),
        in_specs=[
            pl.BlockSpec((seq, LANES), lambda b, h: (b, h)),
            pl.BlockSpec((seq, LANES), lambda b, h: (b, h)),
            pl.BlockSpec((VT_ROWS, seq), lambda b, h: (h, b)),
            pl.BlockSpec((4, DIFF_HEAD_DIM), lambda b, h: (0, 0)),
            pl.BlockSpec((DIFF_V_DIM, 1), lambda b, h: (0, 0)),
        ],
        out_specs=pl.BlockSpec((seq, LANES), lambda b, h: (b, h)),
        out_shape=jax.ShapeDtypeStruct((t, ATTN_V), BF16),
        scratch_shapes=[pltpu.VMEM((2, tq, tq), F32), pltpu.VMEM((2, tq, tq), F32),
                        pltpu.VMEM((2, 1, tq), F32), pltpu.VMEM((2, 1, tq), F32),
                        pltpu.VMEM((2, VT_ROWS, tq), F32), pltpu.VMEM((2, 1, tq), F32)],
        compiler_params=_params("parallel", "parallel"),
    )(q, k, vt, lam_qk, nw_col)


def _layer_norm(r, g, b):
    mu = jnp.mean(r, axis=-1, keepdims=True)
    d = r - mu
    var = jnp.mean(d * d, axis=-1, keepdims=True)
    return d * lax.rsqrt(var + EPS) * g + b


def _outproj_kernel(ys_ref, ya_ref, x_ref, gate_ref, w_ref, g_ref, b_ref, o_ref, *, alpha):
    y = (jnp.dot(ys_ref[...], w_ref[0:SSD_WIDTH, :], preferred_element_type=F32)
         + jnp.dot(ya_ref[...], w_ref[SSD_WIDTH:D_MODEL, :], preferred_element_type=F32))
    r = alpha * x_ref[...] + (1.0 + gate_ref[0]) * y
    o_ref[...] = _layer_norm(r, g_ref[...], b_ref[...])


def _outproj(ys, ya, x2, mod3, w_out, g, b, alpha, seq):
    t, d = x2.shape
    tm = ROW_TILE
    per_seq = seq // tm
    row = lambda i: (i, 0)
    const = lambda i: (0, 0)
    return pl.pallas_call(
        functools.partial(_outproj_kernel, alpha=alpha),
        grid=(t // tm,),
        in_specs=[
            pl.BlockSpec((tm, SSD_WIDTH), row),
            pl.BlockSpec((tm, ATTN_V), row),
            pl.BlockSpec((tm, d), row),
            pl.BlockSpec((1, 1, d), lambda i: ((i // per_seq) * N_MOD + 2, 0, 0)),
            pl.BlockSpec((d, d), const),
            pl.BlockSpec((1, d), const),
            pl.BlockSpec((1, d), const),
        ],
        out_specs=pl.BlockSpec((tm, d), row),
        out_shape=jax.ShapeDtypeStruct((t, d), F32),
        compiler_params=_params("parallel"),
    )(ys, ya, x2, mod3, w_out, g, b)


def _ffn_kernel(x_ref, sc_ref, sh_ref, gate_ref, wg_ref, wu_ref, wd_ref, g_ref, b_ref,
                o_ref, h_ref, acc_ref, *, alpha):
    j = pl.program_id(1)

    @pl.when(j == 0)
    def _():
        h_ref[...] = (x_ref[...] * (1.0 + sc_ref[0]) + sh_ref[0]).astype(BF16)
        acc_ref[...] = jnp.zeros_like(acc_ref)

    h = h_ref[...]
    gt = jnp.dot(h, wg_ref[...], preferred_element_type=F32)
    up = jnp.dot(h, wu_ref[...], preferred_element_type=F32)
    act = (_silu(gt) * up).astype(BF16)
    acc_ref[...] += jnp.dot(act, wd_ref[...], preferred_element_type=F32)

    @pl.when(j == pl.num_programs(1) - 1)
    def _():
        r = alpha * x_ref[...] + (1.0 + gate_ref[0]) * acc_ref[...]
        o_ref[...] = _layer_norm(r, g_ref[...], b_ref[...])


def _ffn(x2, mod3, w_gu, w_down, g, b, alpha, seq):
    t, d = x2.shape
    tm = ROW_TILE
    tf = FF_TILE
    nf = D_FF // tf
    per_seq = seq // tm
    row = lambda i, j: (i, 0)
    const = lambda i, j: (0, 0)
    return pl.pallas_call(
        functools.partial(_ffn_kernel, alpha=alpha),
        grid=(t // tm, nf),
        in_specs=[
            pl.BlockSpec((tm, d), row),
            pl.BlockSpec((1, 1, d), lambda i, j: ((i // per_seq) * N_MOD + 4, 0, 0)),
            pl.BlockSpec((1, 1, d), lambda i, j: ((i // per_seq) * N_MOD + 3, 0, 0)),
            pl.BlockSpec((1, 1, d), lambda i, j: ((i // per_seq) * N_MOD + 5, 0, 0)),
            pl.BlockSpec((d, tf), lambda i, j: (0, j)),
            pl.BlockSpec((d, tf), lambda i, j: (0, nf + j)),
            pl.BlockSpec((tf, d), lambda i, j: (j, 0)),
            pl.BlockSpec((1, d), const),
            pl.BlockSpec((1, d), const),
        ],
        out_specs=pl.BlockSpec((tm, d), row),
        out_shape=jax.ShapeDtypeStruct((t, d), F32),
        scratch_shapes=[pltpu.VMEM((tm, d), BF16), pltpu.VMEM((tm, d), F32)],
        compiler_params=_params("parallel", "arbitrary"),
    )(x2, mod3, mod3, mod3, w_gu, w_gu, w_down, g, b)


def _rope_tables(seq):
    dim = DIFF_HEAD_DIM
    inv = 1.0 / (ROPE_THETA ** (jnp.arange(0, dim, 2, dtype=F32) / dim))
    ang = jnp.arange(seq, dtype=F32)[:, None] * inv[None, :]
    ang = jnp.concatenate([ang, ang, ang, ang], -1)
    sign = jnp.where((jnp.arange(LANES) % dim) < dim // 2, -1.0, 1.0).astype(F32)
    return jnp.cos(ang), jnp.sin(ang) * sign


def _pack_w_in(w):
    z_xbc = w[:, 0:ZX_W]
    dt = w[:, ZX_W:ZX_W + SSD_HEADS]
    qkv = w[:, ZX_W + SSD_HEADS:]
    pad = jnp.zeros((w.shape[0], LANES - SSD_HEADS), w.dtype)
    return jnp.concatenate([z_xbc, qkv, dt, pad], axis=1).astype(BF16)


def _pad_lanes(v, width):
    return jnp.pad(v, (0, width - v.shape[0])).reshape(1, width)


def kernel(x, c, w_mod, b_mod, w_in, conv_w, conv_b, dt_bias, a_log, d_skip, ssd_norm_w, lam_qk,
           attn_norm_w, w_out, ln1_g, ln1_b, w_gate_up, w_down, ln2_g, ln2_b):
    batch, seq, d = x.shape
    depth = w_mod.shape[0]
    assert d == D_MODEL and seq % ROW_TILE == 0 and seq % ATTN_TILE == 0 and batch <= SUBLANES
    t = batch * seq
    alpha = (2 * depth) ** 0.25

    cos2, sin2 = _rope_tables(seq)
    c_pad = jnp.pad(c, ((0, SUBLANES - batch), (0, 0)))
    mod = _modulation(c_pad, w_mod, b_mod)

    x2 = x.reshape(t, d)
    for l in range(depth):
        lambda_init = 0.8 - 0.6 * math.exp(-0.3 * l)
        mod3 = mod[l, :batch].reshape(batch * N_MOD, 1, d)
        zx, dtp, q, k, vt = _inproj(x2, mod3, _pack_w_in(w_in[l]), cos2, sin2, seq)
        y_ssd = _ssd(zx, dtp,
                     jnp.pad(conv_w[l], ((0, SUBLANES - CONV_WIDTH), (0, 0))),
                     conv_b[l].reshape(1, SSD_XBC),
                     _pad_lanes(dt_bias[l], LANES), _pad_lanes(a_log[l], LANES),
                     jnp.repeat(d_skip[l], SSD_HEADDIM).reshape(1, SSD_WIDTH),
                     ssd_norm_w[l].reshape(1, SSD_WIDTH), batch, seq)
        y_attn = _attention(q, k, vt, lam_qk[l], attn_norm_w[l].reshape(DIFF_V_DIM, 1),
                            lambda_init, batch, seq)
        x2 = _outproj(y_ssd, y_attn, x2, mod3, w_out[l].astype(BF16),
                      ln1_g[l].reshape(1, d), ln1_b[l].reshape(1, d), alpha, seq)
        x2 = _ffn(x2, mod3, w_gate_up[l].astype(BF16), w_down[l].astype(BF16),
                  ln2_g[l].reshape(1, d), ln2_b[l].reshape(1, d), alpha, seq)
    return x2.reshape(batch, seq, d)
```

```python
import functools
import math

import jax
import jax.numpy as jnp
from jax import lax
from jax.experimental import pallas as pl
from jax.experimental.pallas import tpu as pltpu

F32 = jnp.float32
BF16 = jnp.bfloat16

D_MODEL = 1024
SSD_WIDTH = 512
SSD_HEADDIM = 64
SSD_HEADS = 8
SSD_GROUPS = 2
SSD_HEADS_PER_GROUP = 4
SSD_STATE = 128
SSD_XBC = SSD_WIDTH + 2 * SSD_GROUPS * SSD_STATE
CONV_WIDTH = 4
CHUNK = 128
DIFF_HEAD_DIM = 64
DIFF_HEADS = 4
DIFF_V_DIM = 128
ATTN_QK = 512
ATTN_V = 512
ROPE_THETA = 10000.0
D_FF = 2816
N_MOD = 6
EPS = 1e-5

LANES = 128
SUBLANES = 8
VMEM_LIMIT_BYTES = 56 * 1024 * 1024

ZX_W = SSD_WIDTH + SSD_XBC
Q_OFF = ZX_W
K_OFF = Q_OFF + ATTN_QK
V_OFF = K_OFF + ATTN_QK
DT_OFF = V_OFF + ATTN_V
IN_PACKED = DT_OFF + LANES

BF16_SUBLANES = 16
VT_ROWS = DIFF_V_DIM + BF16_SUBLANES
LOG2E = math.log2(math.e)

ROW_TILE = 512
ATTN_K_TILE = 512
ATTN_Q_TILE = 2 * ATTN_K_TILE
FF_TILE = 1408
SSD_CHUNKS_PER_STEP = 4


def _sigmoid(x):
    return 1.0 / (1.0 + jnp.exp(-x))


def _silu(x):
    hx = 0.5 * x
    return hx + hx * jnp.tanh(hx)


def _params(*sem):
    return pltpu.CompilerParams(dimension_semantics=sem, vmem_limit_bytes=VMEM_LIMIT_BYTES)


def _mod_kernel(c_ref, w_ref, b_ref, o_ref):
    c = c_ref[...]
    cond = (c * _sigmoid(c)).astype(BF16)
    o_ref[0] = jnp.dot(cond, w_ref[0].astype(BF16), preferred_element_type=F32) + b_ref[0]


def _modulation(c_pad, w_mod, b_mod):
    depth, d, n = w_mod.shape
    tn = 2048
    return pl.pallas_call(
        _mod_kernel,
        grid=(depth, n // tn),
        in_specs=[
            pl.BlockSpec((SUBLANES, d), lambda l, j: (0, 0)),
            pl.BlockSpec((1, d, tn), lambda l, j: (l, 0, j)),
            pl.BlockSpec((1, 1, tn), lambda l, j: (l, 0, j)),
        ],
        out_specs=pl.BlockSpec((1, SUBLANES, tn), lambda l, j: (l, 0, j)),
        out_shape=jax.ShapeDtypeStruct((depth, SUBLANES, n), F32),
        compiler_params=_params("parallel", "parallel"),
    )(c_pad, w_mod, b_mod.reshape(depth, 1, n))


def _inproj_kernel(x_ref, sc_ref, sh_ref, w_ref, cos_ref, sin_ref,
                   zx_ref, dt_ref, qt_ref, k_ref, vt_ref):
    h = (x_ref[...] * (1.0 + sc_ref[0]) + sh_ref[0]).astype(BF16)
    zx_ref[...] = jnp.dot(h, w_ref[:, 0:ZX_W], preferred_element_type=F32)
    cos = cos_ref[...]
    sin = sin_ref[...]
    lane = lax.broadcasted_iota(jnp.int32, cos.shape, 1)
    first_half = (lane % DIFF_HEAD_DIM) < (DIFF_HEAD_DIM // 2)

    def rope(t):
        rot = jnp.where(first_half, pltpu.roll(t, LANES - DIFF_HEAD_DIM // 2, 1),
                        pltpu.roll(t, DIFF_HEAD_DIM // 2, 1))
        return t * cos + rot * sin

    q = jnp.dot(h, w_ref[:, Q_OFF:Q_OFF + ATTN_QK], preferred_element_type=F32)
    k = jnp.dot(h, w_ref[:, K_OFF:K_OFF + ATTN_QK], preferred_element_type=F32)
    vd = jnp.dot(h, w_ref[:, V_OFF:IN_PACKED], preferred_element_type=F32)
    dt_ref[...] = vd[:, ATTN_V:ATTN_V + LANES]
    ones = jnp.ones((BF16_SUBLANES, h.shape[0]), BF16)
    for hd in range(DIFF_HEADS):
        lo = hd * LANES
        qh = rope(q[:, lo:lo + LANES]) * (LOG2E * DIFF_HEAD_DIM ** -0.5)
        k_ref[:, lo:lo + LANES] = rope(k[:, lo:lo + LANES]).astype(BF16)
        qt_ref[lo:lo + LANES, :] = qh.T.astype(BF16)
        vt_ref[hd * VT_ROWS:hd * VT_ROWS + DIFF_V_DIM, :] = vd[:, lo:lo + LANES].T.astype(BF16)
        vt_ref[hd * VT_ROWS + DIFF_V_DIM:(hd + 1) * VT_ROWS, :] = ones


def _inproj(x2, mod3, w_packed, cos2, sin2, seq):
    t, d = x2.shape
    tm = ROW_TILE
    per_seq = seq // tm
    row = lambda i: (i, 0)
    return pl.pallas_call(
        _inproj_kernel,
        grid=(t // tm,),
        in_specs=[
            pl.BlockSpec((tm, d), row),
            pl.BlockSpec((1, 1, d), lambda i: ((i // per_seq) * N_MOD + 1, 0, 0)),
            pl.BlockSpec((1, 1, d), lambda i: ((i // per_seq) * N_MOD + 0, 0, 0)),
            pl.BlockSpec((d, IN_PACKED), lambda i: (0, 0)),
            pl.BlockSpec((tm, LANES), lambda i: (i % per_seq, 0)),
            pl.BlockSpec((tm, LANES), lambda i: (i % per_seq, 0)),
        ],
        out_specs=[
            pl.BlockSpec((tm, ZX_W), row),
            pl.BlockSpec((tm, LANES), row),
            pl.BlockSpec((ATTN_QK, tm), lambda i: (0, i)),
            pl.BlockSpec((tm, ATTN_QK), row),
            pl.BlockSpec((DIFF_HEADS * VT_ROWS, tm), lambda i: (0, i)),
        ],
        out_shape=[
            jax.ShapeDtypeStruct((t, ZX_W), F32),
            jax.ShapeDtypeStruct((t, LANES), F32),
            jax.ShapeDtypeStruct((ATTN_QK, t), BF16),
            jax.ShapeDtypeStruct((t, ATTN_QK), BF16),
            jax.ShapeDtypeStruct((DIFF_HEADS * VT_ROWS, t), BF16),
        ],
        compiler_params=_params("parallel"),
    )(x2, mod3, mod3, w_packed, cos2, sin2)


def _ssd_kernel(zx_ref, dt_ref, cw_ref, cb_ref, dtb_ref, alog_ref, dsk_ref, nw_ref, ex_ref,
                y_ref, xpad, state):
    L = CHUNK

    @pl.when(pl.program_id(1) == 0)
    def _():
        xpad[...] = jnp.zeros_like(xpad)
        state[...] = jnp.zeros_like(state)

    pl.loop(0, SSD_CHUNKS_PER_STEP)(functools.partial(
        _ssd_chunk, zx_ref, dt_ref, cw_ref, cb_ref, dtb_ref, alog_ref, dsk_ref, nw_ref, ex_ref,
        y_ref, xpad, state))


def _split3(x):
    hi = x.astype(BF16)
    r1 = x - hi.astype(F32)
    mid = r1.astype(BF16)
    lo = (r1 - mid.astype(F32)).astype(BF16)
    return hi, mid, lo


def _ssd_chunk(zx_ref, dt_ref, cw_ref, cb_ref, dtb_ref, alog_ref, dsk_ref, nw_ref, ex_ref,
               y_ref, xpad, state, ci):
    L = CHUNK
    rows = pl.ds(pl.multiple_of(ci * L, L), L)
    dot = functools.partial(jnp.dot, preferred_element_type=F32)

    xpad[0:SUBLANES, :] = xpad[L:L + SUBLANES, :]
    xpad[SUBLANES:SUBLANES + L, :] = zx_ref[rows, SSD_WIDTH:ZX_W]
    conv = cb_ref[...]
    for kk in range(CONV_WIDTH):
        off = SUBLANES - (CONV_WIDTH - 1) + kk
        conv = conv + cw_ref[kk:kk + 1, :] * xpad[off:off + L, :]
    xc = _silu(conv)
    xs = xc[:, 0:SSD_WIDTH]

    pre = dt_ref[rows, :] + dtb_ref[...]
    dt = jnp.maximum(pre, 0.0) + jnp.log(1.0 + jnp.exp(-jnp.abs(pre)))
    a = dt * (-jnp.exp(alog_ref[...]))
    row = lax.broadcasted_iota(jnp.int32, (L, L), 0)
    col = lax.broadcasted_iota(jnp.int32, (L, L), 1)
    causal = row >= col
    tril = causal.astype(BF16)
    cs = sum(dot(tril, p) for p in _split3(a))
    cs_t = cs.T
    dt_p = _split3(dt)
    xdt_even = (xs * sum(dot(p, ex_ref[0]) for p in dt_p)).astype(BF16)
    xdt_odd = (xs * sum(dot(p, ex_ref[1]) for p in dt_p)).astype(BF16)
    ecs_x = jnp.exp(sum(dot(p, ex_ref[2]) for p in _split3(cs)))

    ys = []
    for g in range(SSD_GROUPS):
        bm = xc[:, SSD_WIDTH + g * SSD_STATE:SSD_WIDTH + (g + 1) * SSD_STATE]
        cm = xc[:, SSD_WIDTH + (SSD_GROUPS + g) * SSD_STATE:SSD_WIDTH + (SSD_GROUPS + g + 1) * SSD_STATE]
        cm_b = cm.astype(BF16)
        cb = lax.dot_general(cm_b, bm.astype(BF16), (((1,), (1,)), ((), ())),
                             preferred_element_type=F32)
        bm_t = bm.T
        for pp in range(SSD_HEADS_PER_GROUP // 2):
            pr = g * (SSD_HEADS_PER_GROUP // 2) + pp
            halves = (xdt_even[:, pr * LANES:(pr + 1) * LANES],
                      xdt_odd[:, pr * LANES:(pr + 1) * LANES])
            y_pair = None
            st_pair = None
            for half in range(2):
                hd = 2 * pr + half
                cs_col = cs[:, hd:hd + 1]
                cs_row = cs_t[hd:hd + 1, :]
                cs_last = cs_t[hd:hd + 1, L - 1:L]
                lmat = jnp.exp(jnp.where(causal, cs_col - cs_row, -jnp.inf))
                y_h = jnp.dot((cb * lmat).astype(BF16), halves[half], preferred_element_type=F32)
                decay = jnp.exp(cs_last - cs_row)
                st_h = jnp.dot((bm_t * decay).astype(BF16), halves[half],
                               preferred_element_type=F32)
                y_pair = y_h if y_pair is None else y_pair + y_h
                st_pair = st_h if st_pair is None else st_pair + st_h
            prev = state[pr]
            ecs_p = ecs_x[:, pr * LANES:(pr + 1) * LANES]
            y_off = jnp.dot(cm_b, prev.astype(BF16), preferred_element_type=F32) * ecs_p
            state[pr] = ecs_p[L - 1:L, :] * prev + st_pair
            ys.append(y_pair + y_off)

    z = zx_ref[rows, 0:SSD_WIDTH]
    y = (jnp.concatenate(ys, axis=1) + dsk_ref[...] * xs) * _silu(z)
    gw = SSD_WIDTH // SSD_GROUPS
    for g in range(SSD_GROUPS):
        yg = y[:, g * gw:(g + 1) * gw]
        ms = jnp.mean(yg * yg, axis=-1, keepdims=True)
        y_ref[rows, g * gw:(g + 1) * gw] = (yg * lax.rsqrt(ms + EPS)
                                            * nw_ref[:, g * gw:(g + 1) * gw]).astype(BF16)


def _ssd(zx, dtp, conv_w8, conv_b, dtb, alog, dsk, nw, batch, seq):
    t = zx.shape[0]
    rows = CHUNK * SSD_CHUNKS_PER_STEP
    nc = seq // rows
    row = lambda b, c: (b * nc + c, 0)
    const = lambda b, c: (0, 0)
    const3 = lambda b, c: (0, 0, 0)
    head_of_col = jnp.arange(SSD_WIDTH)[None, :] // SSD_HEADDIM
    head = jnp.arange(LANES)[:, None]
    every = head_of_col == head
    expand = jnp.stack([every & (head % 2 == 0), every & (head % 2 == 1), every]).astype(BF16)
    return pl.pallas_call(
        _ssd_kernel,
        grid=(batch, nc),
        in_specs=[
            pl.BlockSpec((rows, ZX_W), row),
            pl.BlockSpec((rows, LANES), row),
            pl.BlockSpec((SUBLANES, SSD_XBC), const),
            pl.BlockSpec((1, SSD_XBC), const),
            pl.BlockSpec((1, LANES), const),
            pl.BlockSpec((1, LANES), const),
            pl.BlockSpec((1, SSD_WIDTH), const),
            pl.BlockSpec((1, SSD_WIDTH), const),
            pl.BlockSpec((3, LANES, SSD_WIDTH), const3),
        ],
        out_specs=pl.BlockSpec((rows, SSD_WIDTH), row),
        out_shape=jax.ShapeDtypeStruct((t, SSD_WIDTH), BF16),
        scratch_shapes=[
            pltpu.VMEM((CHUNK + SUBLANES, SSD_XBC), F32),
            pltpu.VMEM((SSD_HEADS // 2, SSD_STATE, 2 * SSD_HEADDIM), F32),
        ],
        compiler_params=_params("parallel", "arbitrary"),
    )(zx, dtp, conv_w8, conv_b, dtb, alog, dsk, nw, expand)


def _attn_kernel(qt_ref, k_ref, vt_ref, lam_ref, nw_ref, o_ref, sa_ref, sb_ref, mxa_ref, mxb_ref,
                 acc_ref, m_ref, *, lambda_init, nq):
    tq, tk = ATTN_Q_TILE, ATTN_K_TILE
    dv = DIFF_V_DIM
    buf_a = (sa_ref, mxa_ref)
    buf_b = (sb_ref, mxb_ref)
    lq = lam_ref[...]
    lam = (jnp.exp(jnp.sum(lq[0:1] * lq[1:2], axis=-1, keepdims=True))
           - jnp.exp(jnp.sum(lq[2:3] * lq[3:4], axis=-1, keepdims=True)) + lambda_init)

    def finalize(blk):
        r0 = 1.0 / acc_ref[0, dv:dv + 1, :]
        r1 = lam / acc_ref[1, dv:dv + 1, :]
        o = acc_ref[0, 0:dv, :] * r0 - acc_ref[1, 0:dv, :] * r1
        ms = jnp.mean(o * o, axis=0, keepdims=True)
        o = o * lax.rsqrt(ms + EPS) * nw_ref[...] * (1.0 - lambda_init)
        o_ref[pl.ds(pl.multiple_of(blk * tq, tq), tq), :] = o.T.astype(BF16)

    def scores(j, buf, qmaps, q_lo=0):
        s_ref, mx_ref = buf
        kt = k_ref[pl.ds(pl.multiple_of(j * tk, tk), tk), :]
        for m in range(2):
            s = jnp.dot(kt, qmaps[m][:, q_lo:], preferred_element_type=F32)
            s_ref[m, :, q_lo:] = s
            mx_ref[m, :, q_lo:] = jnp.max(s, axis=0, keepdims=True)

    def consume(j, buf, q_lo=0, diagonal=False):
        s_ref, mx_ref = buf
        vt = vt_ref[:, pl.ds(pl.multiple_of(j * tk, tk), tk)]
        for m in range(2):
            s = s_ref[m, :, q_lo:]
            mx = mx_ref[m, :, q_lo:]
            m_old = m_ref[m, :, q_lo:]
            if diagonal:
                krow = lax.broadcasted_iota(jnp.int32, (tk, tk), 0)
                qcol = lax.broadcasted_iota(jnp.int32, (tk, tk), 1)
                sd = jnp.where(krow <= qcol, s[:, 0:tk], -jnp.inf)
                md = jnp.max(sd, axis=0, keepdims=True)
                if s.shape[1] > tk:
                    s = jnp.concatenate([sd, s[:, tk:]], axis=1)
                    mx = jnp.concatenate([md, mx[:, tk:]], axis=1)
                else:
                    s, mx = sd, md
            m_new = jnp.maximum(m_old, mx)
            p = jnp.exp2(s - m_new).astype(BF16)
            acc_ref[m, :, q_lo:] = (jnp.exp2(m_old - m_new) * acc_ref[m, :, q_lo:]
                                    + jnp.dot(vt, p, preferred_element_type=F32))
            m_ref[m, :, q_lo:] = m_new

    acc_ref[...] = jnp.ones_like(acc_ref)

    @pl.loop(0, nq)
    def _(qi):
        qt = qt_ref[:, pl.ds(pl.multiple_of(qi * tq, tq), tq)]
        feat = lax.broadcasted_iota(jnp.int32, qt.shape, 0)
        zero = jnp.zeros_like(qt)
        qmaps = (jnp.where(feat < DIFF_HEAD_DIM, qt, zero), jnp.where(feat >= DIFF_HEAD_DIM, qt, zero))

        finalize(jnp.maximum(qi - 1, 0))
        scores(0, buf_a, qmaps)
        acc_ref[...] = jnp.zeros_like(acc_ref)
        m_ref[...] = jnp.full(m_ref.shape, -jnp.inf, F32)

        @pl.loop(0, qi)
        def _(jj):
            j = 2 * jj
            scores(j + 1, buf_b, qmaps)
            consume(j, buf_a)
            scores(j + 2, buf_a, qmaps)
            consume(j + 1, buf_b)

        scores(2 * qi + 1, buf_b, qmaps, q_lo=tk)
        consume(2 * qi, buf_a, diagonal=True)
        consume(2 * qi + 1, buf_b, q_lo=tk, diagonal=True)

    finalize(nq - 1)


def _attention(qt, k, vt, lam_qk, nw_col, lambda_init, batch, seq):
    t = k.shape[0]
    tq, tk = ATTN_Q_TILE, ATTN_K_TILE
    nq = seq // tq
    return pl.pallas_call(
        functools.partial(_attn_kernel, lambda_init=lambda_init, nq=nq),
        grid=(batch, DIFF_HEADS),
        in_specs=[
            pl.BlockSpec((LANES, seq), lambda b, h: (h, b)),
            pl.BlockSpec((seq, LANES), lambda b, h: (b, h)),
            pl.BlockSpec((VT_ROWS, seq), lambda b, h: (h, b)),
            pl.BlockSpec((4, DIFF_HEAD_DIM), lambda b, h: (0, 0)),
            pl.BlockSpec((DIFF_V_DIM, 1), lambda b, h: (0, 0)),
        ],
        out_specs=pl.BlockSpec((seq, LANES), lambda b, h: (b, h)),
        out_shape=jax.ShapeDtypeStruct((t, ATTN_V), BF16),
        scratch_shapes=[pltpu.VMEM((2, tk, tq), F32), pltpu.VMEM((2, tk, tq), F32),
                        pltpu.VMEM((2, 1, tq), F32), pltpu.VMEM((2, 1, tq), F32),
                        pltpu.VMEM((2, VT_ROWS, tq), F32), pltpu.VMEM((2, 1, tq), F32)],
        compiler_params=_params("parallel", "parallel"),
    )(qt, k, vt, lam_qk, nw_col)


def _layer_norm(r, g, b):
    mu = jnp.mean(r, axis=-1, keepdims=True)
    d = r - mu
    var = jnp.mean(d * d, axis=-1, keepdims=True)
    return d * lax.rsqrt(var + EPS) * g + b


def _outproj_kernel(ys_ref, ya_ref, x_ref, gate_ref, w_ref, g_ref, b_ref, o_ref, *, alpha):
    y = (jnp.dot(ys_ref[...], w_ref[0:SSD_WIDTH, :], preferred_element_type=F32)
         + jnp.dot(ya_ref[...], w_ref[SSD_WIDTH:D_MODEL, :], preferred_element_type=F32))
    r = alpha * x_ref[...] + (1.0 + gate_ref[0]) * y
    o_ref[...] = _layer_norm(r, g_ref[...], b_ref[...])


def _outproj(ys, ya, x2, mod3, w_out, g, b, alpha, seq):
    t, d = x2.shape
    tm = ROW_TILE
    per_seq = seq // tm
    row = lambda i: (i, 0)
    const = lambda i: (0, 0)
    return pl.pallas_call(
        functools.partial(_outproj_kernel, alpha=alpha),
        grid=(t // tm,),
        in_specs=[
            pl.BlockSpec((tm, SSD_WIDTH), row),
            pl.BlockSpec((tm, ATTN_V), row),
            pl.BlockSpec((tm, d), row),
            pl.BlockSpec((1, 1, d), lambda i: ((i // per_seq) * N_MOD + 2, 0, 0)),
            pl.BlockSpec((d, d), const),
            pl.BlockSpec((1, d), const),
            pl.BlockSpec((1, d), const),
        ],
        out_specs=pl.BlockSpec((tm, d), row),
        out_shape=jax.ShapeDtypeStruct((t, d), F32),
        compiler_params=_params("parallel"),
    )(ys, ya, x2, mod3, w_out, g, b)


def _ffn_kernel(x_ref, sc_ref, sh_ref, gate_ref, wg_ref, wu_ref, wd_ref, g_ref, b_ref,
                o_ref, h_ref, acc_ref, *, alpha):
    j = pl.program_id(1)

    @pl.when(j == 0)
    def _():
        h_ref[...] = (x_ref[...] * (1.0 + sc_ref[0]) + sh_ref[0]).astype(BF16)
        acc_ref[...] = jnp.zeros_like(acc_ref)

    h = h_ref[...]
    gt = jnp.dot(h, wg_ref[...], preferred_element_type=F32)
    up = jnp.dot(h, wu_ref[...], preferred_element_type=F32)
    act = (_silu(gt) * up).astype(BF16)
    acc_ref[...] += jnp.dot(act, wd_ref[...], preferred_element_type=F32)

    @pl.when(j == pl.num_programs(1) - 1)
    def _():
        r = alpha * x_ref[...] + (1.0 + gate_ref[0]) * acc_ref[...]
        o_ref[...] = _layer_norm(r, g_ref[...], b_ref[...])


def _ffn(x2, mod3, w_gu, w_down, g, b, alpha, seq):
    t, d = x2.shape
    tm = ROW_TILE
    tf = FF_TILE
    nf = D_FF // tf
    per_seq = seq // tm
    row = lambda i, j: (i, 0)
    const = lambda i, j: (0, 0)
    return pl.pallas_call(
        functools.partial(_ffn_kernel, alpha=alpha),
        grid=(t // tm, nf),
        in_specs=[
            pl.BlockSpec((tm, d), row),
            pl.BlockSpec((1, 1, d), lambda i, j: ((i // per_seq) * N_MOD + 4, 0, 0)),
            pl.BlockSpec((1, 1, d), lambda i, j: ((i // per_seq) * N_MOD + 3, 0, 0)),
            pl.BlockSpec((1, 1, d), lambda i, j: ((i // per_seq) * N_MOD + 5, 0, 0)),
            pl.BlockSpec((d, tf), lambda i, j: (0, j)),
            pl.BlockSpec((d, tf), lambda i, j: (0, nf + j)),
            pl.BlockSpec((tf, d), lambda i, j: (j, 0)),
            pl.BlockSpec((1, d), const),
            pl.BlockSpec((1, d), const),
        ],
        out_specs=pl.BlockSpec((tm, d), row),
        out_shape=jax.ShapeDtypeStruct((t, d), F32),
        scratch_shapes=[pltpu.VMEM((tm, d), BF16), pltpu.VMEM((tm, d), F32)],
        compiler_params=_params("parallel", "arbitrary"),
    )(x2, mod3, mod3, mod3, w_gu, w_gu, w_down, g, b)


def _rope_tables(seq):
    dim = DIFF_HEAD_DIM
    inv = 1.0 / (ROPE_THETA ** (jnp.arange(0, dim, 2, dtype=F32) / dim))
    ang = jnp.arange(seq, dtype=F32)[:, None] * inv[None, :]
    ang = jnp.concatenate([ang, ang, ang, ang], -1)
    sign = jnp.where((jnp.arange(LANES) % dim) < dim // 2, -1.0, 1.0).astype(F32)
    return jnp.cos(ang), jnp.sin(ang) * sign


def _pack_w_in(w):
    z_xbc = w[:, 0:ZX_W]
    dt = w[:, ZX_W:ZX_W + SSD_HEADS]
    qkv = w[:, ZX_W + SSD_HEADS:]
    pad = jnp.zeros((w.shape[0], LANES - SSD_HEADS), w.dtype)
    return jnp.concatenate([z_xbc, qkv, dt, pad], axis=1).astype(BF16)


def _pad_lanes(v, width):
    return jnp.pad(v, (0, width - v.shape[0])).reshape(1, width)


def kernel(x, c, w_mod, b_mod, w_in, conv_w, conv_b, dt_bias, a_log, d_skip, ssd_norm_w, lam_qk,
           attn_norm_w, w_out, ln1_g, ln1_b, w_gate_up, w_down, ln2_g, ln2_b):
    batch, seq, d = x.shape
    depth = w_mod.shape[0]
    assert d == D_MODEL and seq % ROW_TILE == 0 and seq % ATTN_Q_TILE == 0 and batch <= SUBLANES
    t = batch * seq
    alpha = (2 * depth) ** 0.25

    cos2, sin2 = _rope_tables(seq)
    c_pad = jnp.pad(c, ((0, SUBLANES - batch), (0, 0)))
    mod = _modulation(c_pad, w_mod, b_mod)

    x2 = x.reshape(t, d)
    for l in range(depth):
        lambda_init = 0.8 - 0.6 * math.exp(-0.3 * l)
        mod3 = mod[l, :batch].reshape(batch * N_MOD, 1, d)
        zx, dtp, q, k, vt = _inproj(x2, mod3, _pack_w_in(w_in[l]), cos2, sin2, seq)
        y_ssd = _ssd(zx, dtp,
                     jnp.pad(conv_w[l], ((0, SUBLANES - CONV_WIDTH), (0, 0))),
                     conv_b[l].reshape(1, SSD_XBC),
                     _pad_lanes(dt_bias[l], LANES), _pad_lanes(a_log[l], LANES),
                     jnp.repeat(d_skip[l], SSD_HEADDIM).reshape(1, SSD_WIDTH),
                     ssd_norm_w[l].reshape(1, SSD_WIDTH), batch, seq)
        y_attn = _attention(q, k, vt, lam_qk[l], attn_norm_w[l].reshape(DIFF_V_DIM, 1),
                            lambda_init, batch, seq)
        x2 = _outproj(y_ssd, y_attn, x2, mod3, w_out[l].astype(BF16),
                      ln1_g[l].reshape(1, d), ln1_b[l].reshape(1, d), alpha, seq)
        x2 = _ffn(x2, mod3, w_gate_up[l].astype(BF16), w_down[l].astype(BF16),
                  ln2_g[l].reshape(1, d), ln2_b[l].reshape(1, d), alpha, seq)
    return x2.reshape(batch, seq, d)
```

```python
import functools
import math

import jax
import jax.numpy as jnp
from jax import lax
from jax.experimental import pallas as pl
from jax.experimental.pallas import tpu as pltpu

F32 = jnp.float32
BF16 = jnp.bfloat16

D_MODEL = 1024
SSD_WIDTH = 512
SSD_HEADDIM = 64
SSD_HEADS = 8
SSD_GROUPS = 2
SSD_HEADS_PER_GROUP = 4
SSD_STATE = 128
SSD_XBC = SSD_WIDTH + 2 * SSD_GROUPS * SSD_STATE
CONV_WIDTH = 4
CHUNK = 128
DIFF_HEAD_DIM = 64
DIFF_HEADS = 4
DIFF_V_DIM = 128
ATTN_QK = 512
ATTN_V = 512
ROPE_THETA = 10000.0
D_FF = 2816
N_MOD = 6
EPS = 1e-5

LANES = 128
SUBLANES = 8
VMEM_LIMIT_BYTES = 56 * 1024 * 1024

ZX_W = SSD_WIDTH + SSD_XBC
Q_OFF = ZX_W
K_OFF = Q_OFF + ATTN_QK
V_OFF = K_OFF + ATTN_QK
DT_OFF = V_OFF + ATTN_V
IN_PACKED = DT_OFF + LANES

BF16_SUBLANES = 16
VT_ROWS = DIFF_V_DIM + BF16_SUBLANES
LOG2E = math.log2(math.e)

ROW_TILE = 512
ATTN_K_TILE = 512
ATTN_Q_TILE = 2 * ATTN_K_TILE
FF_TILE = 1408
FFN_ROW_TILE = 512
FFN_ROW_SPLIT = 2
SSD_CHUNKS_PER_STEP = 4


def _sigmoid(x):
    return 1.0 / (1.0 + jnp.exp(-x))


def _silu(x):
    hx = 0.5 * x
    return hx + hx * jnp.tanh(hx)


def _params(*sem):
    return pltpu.CompilerParams(dimension_semantics=sem, vmem_limit_bytes=VMEM_LIMIT_BYTES)


def _mod_kernel(c_ref, w_ref, b_ref, o_ref):
    c = c_ref[...]
    cond = (c * _sigmoid(c)).astype(BF16)
    o_ref[0] = jnp.dot(cond, w_ref[0].astype(BF16), preferred_element_type=F32) + b_ref[0]


def _modulation(c_pad, w_mod, b_mod):
    depth, d, n = w_mod.shape
    tn = 2048
    return pl.pallas_call(
        _mod_kernel,
        grid=(depth, n // tn),
        in_specs=[
            pl.BlockSpec((SUBLANES, d), lambda l, j: (0, 0)),
            pl.BlockSpec((1, d, tn), lambda l, j: (l, 0, j)),
            pl.BlockSpec((1, 1, tn), lambda l, j: (l, 0, j)),
        ],
        out_specs=pl.BlockSpec((1, SUBLANES, tn), lambda l, j: (l, 0, j)),
        out_shape=jax.ShapeDtypeStruct((depth, SUBLANES, n), F32),
        compiler_params=_params("parallel", "parallel"),
    )(c_pad, w_mod, b_mod.reshape(depth, 1, n))


def _inproj_kernel(x_ref, sc_ref, sh_ref, w_ref, cos_ref, sin_ref,
                   zx_ref, dt_ref, qt_ref, k_ref, vt_ref):
    h = (x_ref[...] * (1.0 + sc_ref[0]) + sh_ref[0]).astype(BF16)
    zx_ref[...] = jnp.dot(h, w_ref[:, 0:ZX_W], preferred_element_type=F32)
    cos = cos_ref[...]
    sin = sin_ref[...]
    lane = lax.broadcasted_iota(jnp.int32, cos.shape, 1)
    first_half = (lane % DIFF_HEAD_DIM) < (DIFF_HEAD_DIM // 2)

    def rope(t):
        rot = jnp.where(first_half, pltpu.roll(t, LANES - DIFF_HEAD_DIM // 2, 1),
                        pltpu.roll(t, DIFF_HEAD_DIM // 2, 1))
        return t * cos + rot * sin

    q = jnp.dot(h, w_ref[:, Q_OFF:Q_OFF + ATTN_QK], preferred_element_type=F32)
    k = jnp.dot(h, w_ref[:, K_OFF:K_OFF + ATTN_QK], preferred_element_type=F32)
    vd = jnp.dot(h, w_ref[:, V_OFF:IN_PACKED], preferred_element_type=F32)
    dt_ref[...] = vd[:, ATTN_V:ATTN_V + LANES]
    ones = jnp.ones((BF16_SUBLANES, h.shape[0]), BF16)
    for hd in range(DIFF_HEADS):
        lo = hd * LANES
        qh = rope(q[:, lo:lo + LANES]) * (LOG2E * DIFF_HEAD_DIM ** -0.5)
        k_ref[:, lo:lo + LANES] = rope(k[:, lo:lo + LANES]).astype(BF16)
        qt_ref[lo:lo + LANES, :] = qh.T.astype(BF16)
        vt_ref[hd * VT_ROWS:hd * VT_ROWS + DIFF_V_DIM, :] = vd[:, lo:lo + LANES].T.astype(BF16)
        vt_ref[hd * VT_ROWS + DIFF_V_DIM:(hd + 1) * VT_ROWS, :] = ones


def _inproj(x2, mod3, w_packed, cos2, sin2, seq):
    t, d = x2.shape
    tm = ROW_TILE
    per_seq = seq // tm
    row = lambda i: (i, 0)
    return pl.pallas_call(
        _inproj_kernel,
        grid=(t // tm,),
        in_specs=[
            pl.BlockSpec((tm, d), row),
            pl.BlockSpec((1, 1, d), lambda i: ((i // per_seq) * N_MOD + 1, 0, 0)),
            pl.BlockSpec((1, 1, d), lambda i: ((i // per_seq) * N_MOD + 0, 0, 0)),
            pl.BlockSpec((d, IN_PACKED), lambda i: (0, 0)),
            pl.BlockSpec((tm, LANES), lambda i: (i % per_seq, 0)),
            pl.BlockSpec((tm, LANES), lambda i: (i % per_seq, 0)),
        ],
        out_specs=[
            pl.BlockSpec((tm, ZX_W), row),
            pl.BlockSpec((tm, LANES), row),
            pl.BlockSpec((ATTN_QK, tm), lambda i: (0, i)),
            pl.BlockSpec((tm, ATTN_QK), row),
            pl.BlockSpec((DIFF_HEADS * VT_ROWS, tm), lambda i: (0, i)),
        ],
        out_shape=[
            jax.ShapeDtypeStruct((t, ZX_W), F32),
            jax.ShapeDtypeStruct((t, LANES), F32),
            jax.ShapeDtypeStruct((ATTN_QK, t), BF16),
            jax.ShapeDtypeStruct((t, ATTN_QK), BF16),
            jax.ShapeDtypeStruct((DIFF_HEADS * VT_ROWS, t), BF16),
        ],
        compiler_params=_params("parallel"),
    )(x2, mod3, mod3, w_packed, cos2, sin2)


def _ssd_kernel(zx_ref, dt_ref, cw_ref, cb_ref, dtb_ref, alog_ref, dsk_ref, nw_ref, ex_ref,
                y_ref, xpad, state):
    L = CHUNK

    @pl.when(pl.program_id(1) == 0)
    def _():
        xpad[...] = jnp.zeros_like(xpad)
        state[...] = jnp.zeros_like(state)

    pl.loop(0, SSD_CHUNKS_PER_STEP)(functools.partial(
        _ssd_chunk, zx_ref, dt_ref, cw_ref, cb_ref, dtb_ref, alog_ref, dsk_ref, nw_ref, ex_ref,
        y_ref, xpad, state))


def _split3(x):
    hi = x.astype(BF16)
    r1 = x - hi.astype(F32)
    mid = r1.astype(BF16)
    lo = (r1 - mid.astype(F32)).astype(BF16)
    return hi, mid, lo


def _ssd_chunk(zx_ref, dt_ref, cw_ref, cb_ref, dtb_ref, alog_ref, dsk_ref, nw_ref, ex_ref,
               y_ref, xpad, state, ci):
    L = CHUNK
    rows = pl.ds(pl.multiple_of(ci * L, L), L)
    dot = functools.partial(jnp.dot, preferred_element_type=F32)

    u = zx_ref[rows, SSD_WIDTH:ZX_W]

    def taps(v):
        acc = cb_ref[...] + cw_ref[CONV_WIDTH - 1:CONV_WIDTH, :] * v
        for kk in range(CONV_WIDTH - 1):
            acc = acc + cw_ref[kk:kk + 1, :] * pltpu.roll(v, CONV_WIDTH - 1 - kk, 0)
        return acc

    head = taps(jnp.concatenate([xpad[...], u[0:SUBLANES]], axis=0))[SUBLANES:]
    conv = jnp.concatenate([head, taps(u)[SUBLANES:]], axis=0)
    xpad[...] = u[L - SUBLANES:L]
    xc = _silu(conv)
    xs = xc[:, 0:SSD_WIDTH]

    pre = dt_ref[rows, :] + dtb_ref[...]
    dt = jnp.maximum(pre, 0.0) + jnp.log(1.0 + jnp.exp(-jnp.abs(pre)))
    a = dt * (-jnp.exp(alog_ref[...]))
    row = lax.broadcasted_iota(jnp.int32, (L, L), 0)
    col = lax.broadcasted_iota(jnp.int32, (L, L), 1)
    causal = row >= col
    tril = causal.astype(BF16)
    cs = sum(dot(tril, p) for p in _split3(a))
    cs_t = cs.T
    dt_p = _split3(dt)
    xdt_even = (xs * sum(dot(p, ex_ref[0]) for p in dt_p)).astype(BF16)
    xdt_odd = (xs * sum(dot(p, ex_ref[1]) for p in dt_p)).astype(BF16)
    ecs_x = jnp.exp(sum(dot(p, ex_ref[2]) for p in _split3(cs)))

    ys = []
    for g in range(SSD_GROUPS):
        bm = xc[:, SSD_WIDTH + g * SSD_STATE:SSD_WIDTH + (g + 1) * SSD_STATE]
        cm = xc[:, SSD_WIDTH + (SSD_GROUPS + g) * SSD_STATE:SSD_WIDTH + (SSD_GROUPS + g + 1) * SSD_STATE]
        cm_b = cm.astype(BF16)
        cb = lax.dot_general(cm_b, bm.astype(BF16), (((1,), (1,)), ((), ())),
                             preferred_element_type=F32)
        bm_t = bm.T
        for pp in range(SSD_HEADS_PER_GROUP // 2):
            pr = g * (SSD_HEADS_PER_GROUP // 2) + pp
            halves = (xdt_even[:, pr * LANES:(pr + 1) * LANES],
                      xdt_odd[:, pr * LANES:(pr + 1) * LANES])
            y_pair = None
            st_pair = None
            for half in range(2):
                hd = 2 * pr + half
                cs_col = cs[:, hd:hd + 1]
                cs_row = cs_t[hd:hd + 1, :]
                cs_last = cs_t[hd:hd + 1, L - 1:L]
                lmat = jnp.exp(jnp.where(causal, cs_col - cs_row, -jnp.inf))
                y_h = jnp.dot((cb * lmat).astype(BF16), halves[half], preferred_element_type=F32)
                decay = jnp.exp(cs_last - cs_row)
                st_h = jnp.dot((bm_t * decay).astype(BF16), halves[half],
                               preferred_element_type=F32)
                y_pair = y_h if y_pair is None else y_pair + y_h
                st_pair = st_h if st_pair is None else st_pair + st_h
            prev = state[pr]
            ecs_p = ecs_x[:, pr * LANES:(pr + 1) * LANES]
            y_off = jnp.dot(cm_b, prev.astype(BF16), preferred_element_type=F32) * ecs_p
            state[pr] = ecs_p[L - 1:L, :] * prev + st_pair
            ys.append(y_pair + y_off)

    z = zx_ref[rows, 0:SSD_WIDTH]
    y = (jnp.concatenate(ys, axis=1) + dsk_ref[...] * xs) * _silu(z)
    gw = SSD_WIDTH // SSD_GROUPS
    for g in range(SSD_GROUPS):
        yg = y[:, g * gw:(g + 1) * gw]
        ms = jnp.mean(yg * yg, axis=-1, keepdims=True)
        y_ref[rows, g * gw:(g + 1) * gw] = (yg * lax.rsqrt(ms + EPS)
                                            * nw_ref[:, g * gw:(g + 1) * gw]).astype(BF16)


def _ssd(zx, dtp, conv_w8, conv_b, dtb, alog, dsk, nw, batch, seq):
    t = zx.shape[0]
    rows = CHUNK * SSD_CHUNKS_PER_STEP
    nc = seq // rows
    row = lambda b, c: (b * nc + c, 0)
    const = lambda b, c: (0, 0)
    const3 = lambda b, c: (0, 0, 0)
    head_of_col = jnp.arange(SSD_WIDTH)[None, :] // SSD_HEADDIM
    head = jnp.arange(LANES)[:, None]
    every = head_of_col == head
    expand = jnp.stack([every & (head % 2 == 0), every & (head % 2 == 1), every]).astype(BF16)
    return pl.pallas_call(
        _ssd_kernel,
        grid=(batch, nc),
        in_specs=[
            pl.BlockSpec((rows, ZX_W), row),
            pl.BlockSpec((rows, LANES), row),
            pl.BlockSpec((SUBLANES, SSD_XBC), const),
            pl.BlockSpec((1, SSD_XBC), const),
            pl.BlockSpec((1, LANES), const),
            pl.BlockSpec((1, LANES), const),
            pl.BlockSpec((1, SSD_WIDTH), const),
            pl.BlockSpec((1, SSD_WIDTH), const),
            pl.BlockSpec((3, LANES, SSD_WIDTH), const3),
        ],
        out_specs=pl.BlockSpec((rows, SSD_WIDTH), row),
        out_shape=jax.ShapeDtypeStruct((t, SSD_WIDTH), BF16),
        scratch_shapes=[
            pltpu.VMEM((SUBLANES, SSD_XBC), F32),
            pltpu.VMEM((SSD_HEADS // 2, SSD_STATE, 2 * SSD_HEADDIM), F32),
        ],
        compiler_params=_params("parallel", "arbitrary"),
    )(zx, dtp, conv_w8, conv_b, dtb, alog, dsk, nw, expand)


def _attn_kernel(qt_ref, k_ref, vt_ref, lam_ref, nw_ref, o_ref, sa_ref, sb_ref, mxa_ref, mxb_ref,
                 acc_ref, m_ref, *, lambda_init, nq):
    tq, tk = ATTN_Q_TILE, ATTN_K_TILE
    dv = DIFF_V_DIM
    buf_a = (sa_ref, mxa_ref)
    buf_b = (sb_ref, mxb_ref)
    lq = lam_ref[...]
    lam = (jnp.exp(jnp.sum(lq[0:1] * lq[1:2], axis=-1, keepdims=True))
           - jnp.exp(jnp.sum(lq[2:3] * lq[3:4], axis=-1, keepdims=True)) + lambda_init)

    def finalize(blk):
        r0 = 1.0 / acc_ref[0, dv:dv + 1, :]
        r1 = lam / acc_ref[1, dv:dv + 1, :]
        o = acc_ref[0, 0:dv, :] * r0 - acc_ref[1, 0:dv, :] * r1
        ms = jnp.mean(o * o, axis=0, keepdims=True)
        o = o * lax.rsqrt(ms + EPS) * nw_ref[...] * (1.0 - lambda_init)
        o_ref[pl.ds(pl.multiple_of(blk * tq, tq), tq), :] = o.T.astype(BF16)

    def scores(j, buf, qmaps, q_lo=0):
        s_ref, mx_ref = buf
        kt = k_ref[pl.ds(pl.multiple_of(j * tk, tk), tk), :]
        for m in range(2):
            s = jnp.dot(kt, qmaps[m][:, q_lo:], preferred_element_type=F32)
            s_ref[m, :, q_lo:] = s
            mx_ref[m, :, q_lo:] = jnp.max(s, axis=0, keepdims=True)

    def consume(j, buf, q_lo=0, diagonal=False):
        s_ref, mx_ref = buf
        vt = vt_ref[:, pl.ds(pl.multiple_of(j * tk, tk), tk)]
        for m in range(2):
            s = s_ref[m, :, q_lo:]
            mx = mx_ref[m, :, q_lo:]
            m_old = m_ref[m, :, q_lo:]
            if diagonal:
                krow = lax.broadcasted_iota(jnp.int32, (tk, tk), 0)
                qcol = lax.broadcasted_iota(jnp.int32, (tk, tk), 1)
                sd = jnp.where(krow <= qcol, s[:, 0:tk], -jnp.inf)
                md = jnp.max(sd, axis=0, keepdims=True)
                if s.shape[1] > tk:
                    s = jnp.concatenate([sd, s[:, tk:]], axis=1)
                    mx = jnp.concatenate([md, mx[:, tk:]], axis=1)
                else:
                    s, mx = sd, md
            m_new = jnp.maximum(m_old, mx)
            p = jnp.exp2(s - m_new).astype(BF16)
            acc_ref[m, :, q_lo:] = (jnp.exp2(m_old - m_new) * acc_ref[m, :, q_lo:]
                                    + jnp.dot(vt, p, preferred_element_type=F32))
            m_ref[m, :, q_lo:] = m_new

    acc_ref[...] = jnp.ones_like(acc_ref)

    @pl.loop(0, nq)
    def _(qi):
        qt = qt_ref[:, pl.ds(pl.multiple_of(qi * tq, tq), tq)]
        feat = lax.broadcasted_iota(jnp.int32, qt.shape, 0)
        zero = jnp.zeros_like(qt)
        qmaps = (jnp.where(feat < DIFF_HEAD_DIM, qt, zero), jnp.where(feat >= DIFF_HEAD_DIM, qt, zero))

        finalize(jnp.maximum(qi - 1, 0))
        scores(0, buf_a, qmaps)
        acc_ref[...] = jnp.zeros_like(acc_ref)
        m_ref[...] = jnp.full(m_ref.shape, -jnp.inf, F32)

        @pl.loop(0, qi)
        def _(jj):
            j = 2 * jj
            scores(j + 1, buf_b, qmaps)
            consume(j, buf_a)
            scores(j + 2, buf_a, qmaps)
            consume(j + 1, buf_b)

        scores(2 * qi + 1, buf_b, qmaps, q_lo=tk)
        consume(2 * qi, buf_a, diagonal=True)
        consume(2 * qi + 1, buf_b, q_lo=tk, diagonal=True)

    finalize(nq - 1)


def _attention(qt, k, vt, lam_qk, nw_col, lambda_init, batch, seq):
    t = k.shape[0]
    tq, tk = ATTN_Q_TILE, ATTN_K_TILE
    nq = seq // tq
    return pl.pallas_call(
        functools.partial(_attn_kernel, lambda_init=lambda_init, nq=nq),
        grid=(batch, DIFF_HEADS),
        in_specs=[
            pl.BlockSpec((LANES, seq), lambda b, h: (h, b)),
            pl.BlockSpec((seq, LANES), lambda b, h: (b, h)),
            pl.BlockSpec((VT_ROWS, seq), lambda b, h: (h, b)),
            pl.BlockSpec((4, DIFF_HEAD_DIM), lambda b, h: (0, 0)),
            pl.BlockSpec((DIFF_V_DIM, 1), lambda b, h: (0, 0)),
        ],
        out_specs=pl.BlockSpec((seq, LANES), lambda b, h: (b, h)),
        out_shape=jax.ShapeDtypeStruct((t, ATTN_V), BF16),
        scratch_shapes=[pltpu.VMEM((2, tk, tq), F32), pltpu.VMEM((2, tk, tq), F32),
                        pltpu.VMEM((2, 1, tq), F32), pltpu.VMEM((2, 1, tq), F32),
                        pltpu.VMEM((2, VT_ROWS, tq), F32), pltpu.VMEM((2, 1, tq), F32)],
        compiler_params=_params("parallel", "parallel"),
    )(qt, k, vt, lam_qk, nw_col)


def _layer_norm(r, g, b):
    mu = jnp.mean(r, axis=-1, keepdims=True)
    d = r - mu
    var = jnp.mean(d * d, axis=-1, keepdims=True)
    return d * lax.rsqrt(var + EPS) * g + b


def _mix_ffn_kernel(ys_ref, ya_ref, x_ref, g1_ref, sc2_ref, sh2_ref, g2_ref, wo_ref, wgu_ref, wd_ref,
                    n1g_ref, n1b_ref, n2g_ref, n2b_ref, o_ref, *, alpha):
    tm = x_ref.shape[0]
    halves = [pl.ds(s * (tm // FFN_ROW_SPLIT), tm // FFN_ROW_SPLIT) for s in range(FFN_ROW_SPLIT)]
    dot = functools.partial(jnp.dot, preferred_element_type=F32)

    def outproj(rows):
        return (dot(ys_ref[rows, :], wo_ref[0:SSD_WIDTH, :])
                + dot(ya_ref[rows, :], wo_ref[SSD_WIDTH:D_MODEL, :]))

    def norm_modulate(y, rows):
        x1 = _layer_norm(alpha * x_ref[rows, :] + (1.0 + g1_ref[0]) * y, n1g_ref[...], n1b_ref[...])
        return x1, (x1 * (1.0 + sc2_ref[0]) + sh2_ref[0]).astype(BF16)

    def swiglu(h):
        acc = None
        for c in range(D_FF // FF_TILE):
            lo = c * FF_TILE
            gt = dot(h, wgu_ref[:, lo:lo + FF_TILE])
            up = dot(h, wgu_ref[:, D_FF + lo:D_FF + lo + FF_TILE])
            part = dot((_silu(gt) * up).astype(BF16), wd_ref[lo:lo + FF_TILE, :])
            acc = part if acc is None else acc + part
        return acc

    ys = [outproj(rows) for rows in halves]
    outs = []
    for y, rows in zip(ys, halves):
        x1, h = norm_modulate(y, rows)
        outs.append((x1, swiglu(h)))
    for (x1, acc), rows in zip(outs, halves):
        o_ref[rows, :] = _layer_norm(alpha * x1 + (1.0 + g2_ref[0]) * acc, n2g_ref[...], n2b_ref[...])


def _mix_ffn(ys, ya, x2, mod3, w_out, w_gu, w_down, n1g, n1b, n2g, n2b, alpha, seq):
    t, d = x2.shape
    tm = FFN_ROW_TILE
    per_seq = seq // tm
    row = lambda i: (i, 0)
    const = lambda i: (0, 0)
    mod_row = lambda k: pl.BlockSpec((1, 1, d), lambda i: ((i // per_seq) * N_MOD + k, 0, 0))
    resident = lambda shape: pl.BlockSpec(shape, const, pipeline_mode=pl.Buffered(1))
    return pl.pallas_call(
        functools.partial(_mix_ffn_kernel, alpha=alpha),
        grid=(t // tm,),
        in_specs=[
            pl.BlockSpec((tm, SSD_WIDTH), row),
            pl.BlockSpec((tm, ATTN_V), row),
            pl.BlockSpec((tm, d), row),
            mod_row(2), mod_row(4), mod_row(3), mod_row(5),
            resident((d, d)),
            resident((d, 2 * D_FF)),
            resident((D_FF, d)),
            pl.BlockSpec((1, d), const), pl.BlockSpec((1, d), const),
            pl.BlockSpec((1, d), const), pl.BlockSpec((1, d), const),
        ],
        out_specs=pl.BlockSpec((tm, d), row),
        out_shape=jax.ShapeDtypeStruct((t, d), F32),
        compiler_params=_params("parallel"),
    )(ys, ya, x2, mod3, mod3, mod3, mod3, w_out, w_gu, w_down, n1g, n1b, n2g, n2b)


def _rope_tables(seq):
    dim = DIFF_HEAD_DIM
    inv = 1.0 / (ROPE_THETA ** (jnp.arange(0, dim, 2, dtype=F32) / dim))
    ang = jnp.arange(seq, dtype=F32)[:, None] * inv[None, :]
    cos, sin = jnp.cos(ang), jnp.sin(ang)
    return (jnp.concatenate([cos, cos, cos, cos], -1), jnp.concatenate([-sin, sin, -sin, sin], -1))


def _pack_w_in(w):
    z_xbc = w[..., 0:ZX_W]
    dt = w[..., ZX_W:ZX_W + SSD_HEADS]
    qkv = w[..., ZX_W + SSD_HEADS:]
    pad = jnp.zeros(w.shape[:-1] + (LANES - SSD_HEADS,), w.dtype)
    return jnp.concatenate([z_xbc, qkv, dt, pad], axis=-1).astype(BF16)


def _pad_lanes(v, width):
    return jnp.pad(v, (0, width - v.shape[0])).reshape(1, width)


def kernel(x, c, w_mod, b_mod, w_in, conv_w, conv_b, dt_bias, a_log, d_skip, ssd_norm_w, lam_qk,
           attn_norm_w, w_out, ln1_g, ln1_b, w_gate_up, w_down, ln2_g, ln2_b):
    batch, seq, d = x.shape
    depth = w_mod.shape[0]
    assert d == D_MODEL and seq % ROW_TILE == 0 and seq % ATTN_Q_TILE == 0 and batch <= SUBLANES
    t = batch * seq
    alpha = (2 * depth) ** 0.25

    cos2, sin2 = _rope_tables(seq)
    c_pad = jnp.pad(c, ((0, SUBLANES - batch), (0, 0)))
    mod = _modulation(c_pad, w_mod, b_mod)
    w_in_packed = _pack_w_in(w_in)

    x2 = x.reshape(t, d)
    for l in range(depth):
        lambda_init = 0.8 - 0.6 * math.exp(-0.3 * l)
        mod3 = mod[l, :batch].reshape(batch * N_MOD, 1, d)
        zx, dtp, q, k, vt = _inproj(x2, mod3, w_in_packed[l], cos2, sin2, seq)
        y_ssd = _ssd(zx, dtp,
                     jnp.pad(conv_w[l], ((0, SUBLANES - CONV_WIDTH), (0, 0))),
                     conv_b[l].reshape(1, SSD_XBC),
                     _pad_lanes(dt_bias[l], LANES), _pad_lanes(a_log[l], LANES),
                     jnp.repeat(d_skip[l], SSD_HEADDIM).reshape(1, SSD_WIDTH),
                     ssd_norm_w[l].reshape(1, SSD_WIDTH), batch, seq)
        y_attn = _attention(q, k, vt, lam_qk[l], attn_norm_w[l].reshape(DIFF_V_DIM, 1),
                            lambda_init, batch, seq)
        x2 = _mix_ffn(y_ssd, y_attn, x2, mod3, w_out[l].astype(BF16),
                      w_gate_up[l].astype(BF16), w_down[l].astype(BF16),
                      ln1_g[l].reshape(1, d), ln1_b[l].reshape(1, d),
                      ln2_g[l].reshape(1, d), ln2_b[l].reshape(1, d), alpha, seq)
    return x2.reshape(batch, seq, d)
```

```python
import functools
import math

import jax
import jax.numpy as jnp
from jax import lax
from jax.experimental import pallas as pl
from jax.experimental.pallas import tpu as pltpu

F32 = jnp.float32
BF16 = jnp.bfloat16

D_MODEL = 1024
SSD_WIDTH = 512
SSD_HEADDIM = 64
SSD_HEADS = 8
SSD_GROUPS = 2
SSD_HEADS_PER_GROUP = 4
SSD_STATE = 128
SSD_XBC = SSD_WIDTH + 2 * SSD_GROUPS * SSD_STATE
CONV_WIDTH = 4
CHUNK = 128
DIFF_HEAD_DIM = 64
DIFF_HEADS = 4
DIFF_V_DIM = 128
ATTN_QK = 512
ATTN_V = 512
ROPE_THETA = 10000.0
D_FF = 2816
N_MOD = 6
EPS = 1e-5

LANES = 128
SUBLANES = 8
VMEM_LIMIT_BYTES = 56 * 1024 * 1024

ZX_W = SSD_WIDTH + SSD_XBC
DT_LANE0 = LANES - SSD_HEADS
Q_OFF_B = LANES
K_OFF_B = Q_OFF_B + ATTN_QK
V_OFF_B = K_OFF_B + ATTN_QK
WB_W = V_OFF_B + ATTN_V

BF16_SUBLANES = 16
VT_ROWS = DIFF_V_DIM + BF16_SUBLANES
LOG2E = math.log2(math.e)

ROW_TILE = 512
ATTN_K_TILE = 512
ATTN_Q_TILE = 2 * ATTN_K_TILE
FF_TILE = 1408
FFN_ROW_TILE = 512
FFN_ROW_SPLIT = 2


def _sigmoid(x):
    return 1.0 / (1.0 + jnp.exp(-x))


def _silu(x):
    hx = 0.5 * x
    return hx + hx * jnp.tanh(hx)


def _params(*sem):
    return pltpu.CompilerParams(dimension_semantics=sem, vmem_limit_bytes=VMEM_LIMIT_BYTES)


def _mod_kernel(c_ref, w_ref, b_ref, o_ref):
    c = c_ref[...]
    cond = (c * _sigmoid(c)).astype(BF16)
    o_ref[0] = jnp.dot(cond, w_ref[0].astype(BF16), preferred_element_type=F32) + b_ref[0]


def _modulation(c_pad, w_mod, b_mod):
    depth, d, n = w_mod.shape
    tn = 2048
    return pl.pallas_call(
        _mod_kernel,
        grid=(depth, n // tn),
        in_specs=[
            pl.BlockSpec((SUBLANES, d), lambda l, j: (0, 0)),
            pl.BlockSpec((1, d, tn), lambda l, j: (l, 0, j)),
            pl.BlockSpec((1, 1, tn), lambda l, j: (l, 0, j)),
        ],
        out_specs=pl.BlockSpec((1, SUBLANES, tn), lambda l, j: (l, 0, j)),
        out_shape=jax.ShapeDtypeStruct((depth, SUBLANES, n), F32),
        compiler_params=_params("parallel", "parallel"),
    )(c_pad, w_mod, b_mod.reshape(depth, 1, n))


def _split(x, pieces):
    out = []
    for _ in range(pieces):
        p = x.astype(BF16)
        out.append(p)
        x = x - p.astype(F32)
    return out


def _ssd_chunk(u, z, dt_raw, cw_ref, cb_ref, dtb_ref, alog_ref, dsk_ref, nw_ref, ex_ref, xpad, state):
    L = CHUNK
    dot = functools.partial(jnp.dot, preferred_element_type=F32)

    def taps(v):
        acc = cb_ref[...] + cw_ref[CONV_WIDTH - 1:CONV_WIDTH, :] * v
        for kk in range(CONV_WIDTH - 1):
            acc = acc + cw_ref[kk:kk + 1, :] * pltpu.roll(v, CONV_WIDTH - 1 - kk, 0)
        return acc

    head = taps(jnp.concatenate([xpad[...], u[0:SUBLANES]], axis=0))[SUBLANES:]
    conv = jnp.concatenate([head, taps(u)[SUBLANES:]], axis=0)
    xpad[...] = u[L - SUBLANES:L]
    xc = _silu(conv)
    xs = xc[:, 0:SSD_WIDTH]

    pre = dt_raw + dtb_ref[...]
    dt = jnp.maximum(pre, 0.0) + jnp.log(1.0 + jnp.exp(-jnp.abs(pre)))
    a = dt * (-jnp.exp(alog_ref[...]))
    row = lax.broadcasted_iota(jnp.int32, (L, L), 0)
    col = lax.broadcasted_iota(jnp.int32, (L, L), 1)
    causal = row >= col
    tril = causal.astype(BF16)
    cs = sum(dot(tril, p) for p in _split(a, 3))
    cs_t = cs.T
    expand = lambda v: sum(dot(p, ex_ref[...]) for p in _split(v, 2))
    xdt = xs * expand(dt)
    ecs_x = jnp.exp(expand(cs))
    even_head = (lax.broadcasted_iota(jnp.int32, xdt.shape, 1) // SSD_HEADDIM) % 2 == 0
    xdt_even = jnp.where(even_head, xdt, 0.0).astype(BF16)
    xdt_odd = jnp.where(even_head, 0.0, xdt).astype(BF16)

    ys = []
    for g in range(SSD_GROUPS):
        bm = xc[:, SSD_WIDTH + g * SSD_STATE:SSD_WIDTH + (g + 1) * SSD_STATE]
        cm = xc[:, SSD_WIDTH + (SSD_GROUPS + g) * SSD_STATE:SSD_WIDTH + (SSD_GROUPS + g + 1) * SSD_STATE]
        cm_b = cm.astype(BF16)
        cb = lax.dot_general(cm_b, bm.astype(BF16), (((1,), (1,)), ((), ())),
                             preferred_element_type=F32)
        bm_t = bm.T
        for pp in range(SSD_HEADS_PER_GROUP // 2):
            pr = g * (SSD_HEADS_PER_GROUP // 2) + pp
            halves = (xdt_even[:, pr * LANES:(pr + 1) * LANES],
                      xdt_odd[:, pr * LANES:(pr + 1) * LANES])
            y_pair = None
            st_pair = None
            for half in range(2):
                hd = DT_LANE0 + 2 * pr + half
                cs_col = cs[:, hd:hd + 1]
                cs_row = cs_t[hd:hd + 1, :]
                cs_last = cs_t[hd:hd + 1, L - 1:L]
                lmat = jnp.exp(jnp.where(causal, cs_col - cs_row, -jnp.inf))
                y_h = jnp.dot((cb * lmat).astype(BF16), halves[half], preferred_element_type=F32)
                decay = jnp.exp(cs_last - cs_row)
                st_h = jnp.dot((bm_t * decay).astype(BF16), halves[half],
                               preferred_element_type=F32)
                y_pair = y_h if y_pair is None else y_pair + y_h
                st_pair = st_h if st_pair is None else st_pair + st_h
            prev = state[pr]
            ecs_p = ecs_x[:, pr * LANES:(pr + 1) * LANES]
            y_off = jnp.dot(cm_b, prev.astype(BF16), preferred_element_type=F32) * ecs_p
            state[pr] = ecs_p[L - 1:L, :] * prev + st_pair
            ys.append(y_pair + y_off)

    y = (jnp.concatenate(ys, axis=1) + dsk_ref[...] * xs) * _silu(z)
    gw = SSD_WIDTH // SSD_GROUPS
    out = []
    for g in range(SSD_GROUPS):
        yg = y[:, g * gw:(g + 1) * gw]
        ms = jnp.mean(yg * yg, axis=-1, keepdims=True)
        out.append((yg * lax.rsqrt(ms + EPS) * nw_ref[:, g * gw:(g + 1) * gw]).astype(BF16))
    return jnp.concatenate(out, axis=1)


def _inproj_ssd_kernel(x_ref, sc_ref, sh_ref, wa_ref, wb_ref, cos_ref, sin_ref,
                       cw_ref, cb_ref, dtb_ref, alog_ref, dsk_ref, nw_ref, ex_ref,
                       y_ref, qt_ref, k_ref, vt_ref, xpad, state, *, per_seq):
    @pl.when(pl.program_id(0) % per_seq == 0)
    def _():
        xpad[...] = jnp.zeros_like(xpad)
        state[...] = jnp.zeros_like(state)

    dot = functools.partial(jnp.dot, preferred_element_type=F32)
    h = (x_ref[...] * (1.0 + sc_ref[0]) + sh_ref[0]).astype(BF16)
    cos = cos_ref[...]
    sin = sin_ref[...]
    lane = lax.broadcasted_iota(jnp.int32, cos.shape, 1)
    first_half = (lane % DIFF_HEAD_DIM) < (DIFF_HEAD_DIM // 2)

    def rope(t):
        rot = jnp.where(first_half, pltpu.roll(t, LANES - DIFF_HEAD_DIM // 2, 1),
                        pltpu.roll(t, DIFF_HEAD_DIM // 2, 1))
        return t * cos + rot * sin

    ssd = functools.partial(_ssd_chunk, cw_ref=cw_ref, cb_ref=cb_ref, dtb_ref=dtb_ref,
                            alog_ref=alog_ref, dsk_ref=dsk_ref, nw_ref=nw_ref, ex_ref=ex_ref,
                            xpad=xpad, state=state)

    def ssd_rows(c):
        rows = slice(c * CHUNK, (c + 1) * CHUNK)
        y_ref[rows, :] = ssd(xbc[rows], z[rows], dtq[rows, 0:LANES])

    xbc = dot(h, wa_ref[:, SSD_WIDTH:ZX_W])
    dtq = dot(h, wb_ref[:, 0:K_OFF_B])
    z = dot(h, wa_ref[:, 0:SSD_WIDTH])
    ssd_rows(0)
    k = dot(h, wb_ref[:, K_OFF_B:V_OFF_B])
    ssd_rows(1)
    v = dot(h, wb_ref[:, V_OFF_B:WB_W])
    ssd_rows(2)
    ones = jnp.ones((BF16_SUBLANES, h.shape[0]), BF16)
    for hd in range(DIFF_HEADS):
        lo = hd * LANES
        qh = rope(dtq[:, Q_OFF_B + lo:Q_OFF_B + lo + LANES]) * (LOG2E * DIFF_HEAD_DIM ** -0.5)
        k_ref[:, lo:lo + LANES] = rope(k[:, lo:lo + LANES]).astype(BF16)
        qt_ref[lo:lo + LANES, :] = qh.T.astype(BF16)
        vt_ref[hd * VT_ROWS:hd * VT_ROWS + DIFF_V_DIM, :] = v[:, lo:lo + LANES].T.astype(BF16)
        vt_ref[hd * VT_ROWS + DIFF_V_DIM:(hd + 1) * VT_ROWS, :] = ones
    ssd_rows(3)


def _inproj_ssd(x2, mod3, w_a, w_b, cos2, sin2, conv_w8, conv_b, dtb, alog, dsk, nw, seq):
    t, d = x2.shape
    tm = ROW_TILE
    assert tm == 4 * CHUNK
    per_seq = seq // tm
    row = lambda i: (i, 0)
    const = lambda i: (0, 0)
    head_of_col = jnp.arange(SSD_WIDTH)[None, :] // SSD_HEADDIM
    head = jnp.arange(LANES)[:, None] - DT_LANE0
    expand = (head_of_col == head).astype(BF16)
    return pl.pallas_call(
        functools.partial(_inproj_ssd_kernel, per_seq=per_seq),
        grid=(t // tm,),
        in_specs=[
            pl.BlockSpec((tm, d), row),
            pl.BlockSpec((1, 1, d), lambda i: ((i // per_seq) * N_MOD + 1, 0, 0)),
            pl.BlockSpec((1, 1, d), lambda i: ((i // per_seq) * N_MOD + 0, 0, 0)),
            pl.BlockSpec((d, ZX_W), const),
            pl.BlockSpec((d, WB_W), const),
            pl.BlockSpec((tm, LANES), lambda i: (i % per_seq, 0)),
            pl.BlockSpec((tm, LANES), lambda i: (i % per_seq, 0)),
            pl.BlockSpec((SUBLANES, SSD_XBC), const),
            pl.BlockSpec((1, SSD_XBC), const),
            pl.BlockSpec((1, LANES), const),
            pl.BlockSpec((1, LANES), const),
            pl.BlockSpec((1, SSD_WIDTH), const),
            pl.BlockSpec((1, SSD_WIDTH), const),
            pl.BlockSpec((LANES, SSD_WIDTH), const),
        ],
        out_specs=[
            pl.BlockSpec((tm, SSD_WIDTH), row),
            pl.BlockSpec((ATTN_QK, tm), lambda i: (0, i)),
            pl.BlockSpec((tm, ATTN_QK), row),
            pl.BlockSpec((DIFF_HEADS * VT_ROWS, tm), lambda i: (0, i)),
        ],
        out_shape=[
            jax.ShapeDtypeStruct((t, SSD_WIDTH), BF16),
            jax.ShapeDtypeStruct((ATTN_QK, t), BF16),
            jax.ShapeDtypeStruct((t, ATTN_QK), BF16),
            jax.ShapeDtypeStruct((DIFF_HEADS * VT_ROWS, t), BF16),
        ],
        scratch_shapes=[
            pltpu.VMEM((SUBLANES, SSD_XBC), F32),
            pltpu.VMEM((SSD_HEADS // 2, SSD_STATE, 2 * SSD_HEADDIM), F32),
        ],
        compiler_params=_params("arbitrary"),
    )(x2, mod3, mod3, w_a, w_b, cos2, sin2, conv_w8, conv_b, dtb, alog, dsk, nw, expand)


def _attn_kernel(qt_ref, k_ref, vt_ref, lam_ref, nw_ref, o_ref, sa_ref, sb_ref, mxa_ref, mxb_ref,
                 acc_ref, m_ref, *, lambda_init, nq):
    tq, tk = ATTN_Q_TILE, ATTN_K_TILE
    dv = DIFF_V_DIM
    buf_a = (sa_ref, mxa_ref)
    buf_b = (sb_ref, mxb_ref)
    lq = lam_ref[...]
    lam = (jnp.exp(jnp.sum(lq[0:1] * lq[1:2], axis=-1, keepdims=True))
           - jnp.exp(jnp.sum(lq[2:3] * lq[3:4], axis=-1, keepdims=True)) + lambda_init)

    def finalize(blk):
        r0 = 1.0 / acc_ref[0, dv:dv + 1, :]
        r1 = lam / acc_ref[1, dv:dv + 1, :]
        o = acc_ref[0, 0:dv, :] * r0 - acc_ref[1, 0:dv, :] * r1
        ms = jnp.mean(o * o, axis=0, keepdims=True)
        o = o * lax.rsqrt(ms + EPS) * nw_ref[...] * (1.0 - lambda_init)
        o_ref[pl.ds(pl.multiple_of(blk * tq, tq), tq), :] = o.T.astype(BF16)

    def scores(j, buf, qmaps, q_lo=0):
        s_ref, mx_ref = buf
        kt = k_ref[pl.ds(pl.multiple_of(j * tk, tk), tk), :]
        for m in range(2):
            s = jnp.dot(kt, qmaps[m][:, q_lo:], preferred_element_type=F32)
            s_ref[m, :, q_lo:] = s
            mx_ref[m, :, q_lo:] = jnp.max(s, axis=0, keepdims=True)

    def consume(j, buf, q_lo=0, diagonal=False):
        s_ref, mx_ref = buf
        vt = vt_ref[:, pl.ds(pl.multiple_of(j * tk, tk), tk)]
        for m in range(2):
            s = s_ref[m, :, q_lo:]
            mx = mx_ref[m, :, q_lo:]
            m_old = m_ref[m, :, q_lo:]
            if diagonal:
                krow = lax.broadcasted_iota(jnp.int32, (tk, tk), 0)
                qcol = lax.broadcasted_iota(jnp.int32, (tk, tk), 1)
                sd = jnp.where(krow <= qcol, s[:, 0:tk], -jnp.inf)
                md = jnp.max(sd, axis=0, keepdims=True)
                if s.shape[1] > tk:
                    s = jnp.concatenate([sd, s[:, tk:]], axis=1)
                    mx = jnp.concatenate([md, mx[:, tk:]], axis=1)
                else:
                    s, mx = sd, md
            m_new = jnp.maximum(m_old, mx)
            p = jnp.exp2(s - m_new).astype(BF16)
            acc_ref[m, :, q_lo:] = (jnp.exp2(m_old - m_new) * acc_ref[m, :, q_lo:]
                                    + jnp.dot(vt, p, preferred_element_type=F32))
            m_ref[m, :, q_lo:] = m_new

    acc_ref[...] = jnp.ones_like(acc_ref)

    @pl.loop(0, nq)
    def _(qi):
        qt = qt_ref[:, pl.ds(pl.multiple_of(qi * tq, tq), tq)]
        feat = lax.broadcasted_iota(jnp.int32, qt.shape, 0)
        zero = jnp.zeros_like(qt)
        qmaps = (jnp.where(feat < DIFF_HEAD_DIM, qt, zero), jnp.where(feat >= DIFF_HEAD_DIM, qt, zero))

        finalize(jnp.maximum(qi - 1, 0))
        scores(0, buf_a, qmaps)
        acc_ref[...] = jnp.zeros_like(acc_ref)
        m_ref[...] = jnp.full(m_ref.shape, -jnp.inf, F32)

        @pl.loop(0, qi)
        def _(jj):
            j = 2 * jj
            scores(j + 1, buf_b, qmaps)
            consume(j, buf_a)
            scores(j + 2, buf_a, qmaps)
            consume(j + 1, buf_b)

        scores(2 * qi + 1, buf_b, qmaps, q_lo=tk)
        consume(2 * qi, buf_a, diagonal=True)
        consume(2 * qi + 1, buf_b, q_lo=tk, diagonal=True)

    finalize(nq - 1)


def _attention(qt, k, vt, lam_qk, nw_col, lambda_init, batch, seq):
    t = k.shape[0]
    tq, tk = ATTN_Q_TILE, ATTN_K_TILE
    nq = seq // tq
    return pl.pallas_call(
        functools.partial(_attn_kernel, lambda_init=lambda_init, nq=nq),
        grid=(batch, DIFF_HEADS),
        in_specs=[
            pl.BlockSpec((LANES, seq), lambda b, h: (h, b)),
            pl.BlockSpec((seq, LANES), lambda b, h: (b, h)),
            pl.BlockSpec((VT_ROWS, seq), lambda b, h: (h, b)),
            pl.BlockSpec((4, DIFF_HEAD_DIM), lambda b, h: (0, 0)),
            pl.BlockSpec((DIFF_V_DIM, 1), lambda b, h: (0, 0)),
        ],
        out_specs=pl.BlockSpec((seq, LANES), lambda b, h: (b, h)),
        out_shape=jax.ShapeDtypeStruct((t, ATTN_V), BF16),
        scratch_shapes=[pltpu.VMEM((2, tk, tq), F32), pltpu.VMEM((2, tk, tq), F32),
                        pltpu.VMEM((2, 1, tq), F32), pltpu.VMEM((2, 1, tq), F32),
                        pltpu.VMEM((2, VT_ROWS, tq), F32), pltpu.VMEM((2, 1, tq), F32)],
        compiler_params=_params("parallel", "parallel"),
    )(qt, k, vt, lam_qk, nw_col)


def _layer_norm(r, g, b):
    mu = jnp.mean(r, axis=-1, keepdims=True)
    d = r - mu
    var = jnp.mean(d * d, axis=-1, keepdims=True)
    return d * lax.rsqrt(var + EPS) * g + b


def _mix_ffn_kernel(ys_ref, ya_ref, x_ref, g1_ref, sc2_ref, sh2_ref, g2_ref, wo_ref, wgu_ref, wd_ref,
                    n1g_ref, n1b_ref, n2g_ref, n2b_ref, o_ref, *, alpha):
    tm = x_ref.shape[0]
    halves = [pl.ds(s * (tm // FFN_ROW_SPLIT), tm // FFN_ROW_SPLIT) for s in range(FFN_ROW_SPLIT)]
    dot = functools.partial(jnp.dot, preferred_element_type=F32)

    def outproj(rows):
        return (dot(ys_ref[rows, :], wo_ref[0:SSD_WIDTH, :])
                + dot(ya_ref[rows, :], wo_ref[SSD_WIDTH:D_MODEL, :]))

    def norm_modulate(y, rows):
        x1 = _layer_norm(alpha * x_ref[rows, :] + (1.0 + g1_ref[0]) * y, n1g_ref[...], n1b_ref[...])
        return x1, (x1 * (1.0 + sc2_ref[0]) + sh2_ref[0]).astype(BF16)

    def swiglu(h):
        acc = None
        for c in range(D_FF // FF_TILE):
            lo = c * FF_TILE
            gt = dot(h, wgu_ref[:, lo:lo + FF_TILE])
            up = dot(h, wgu_ref[:, D_FF + lo:D_FF + lo + FF_TILE])
            part = dot((_silu(gt) * up).astype(BF16), wd_ref[lo:lo + FF_TILE, :])
            acc = part if acc is None else acc + part
        return acc

    ys = [outproj(rows) for rows in halves]
    outs = []
    for y, rows in zip(ys, halves):
        x1, h = norm_modulate(y, rows)
        outs.append((x1, swiglu(h)))
    for (x1, acc), rows in zip(outs, halves):
        o_ref[rows, :] = _layer_norm(alpha * x1 + (1.0 + g2_ref[0]) * acc, n2g_ref[...], n2b_ref[...])


def _mix_ffn(ys, ya, x2, mod3, w_out, w_gu, w_down, n1g, n1b, n2g, n2b, alpha, seq):
    t, d = x2.shape
    tm = FFN_ROW_TILE
    per_seq = seq // tm
    row = lambda i: (i, 0)
    const = lambda i: (0, 0)
    mod_row = lambda k: pl.BlockSpec((1, 1, d), lambda i: ((i // per_seq) * N_MOD + k, 0, 0))
    resident = lambda shape: pl.BlockSpec(shape, const, pipeline_mode=pl.Buffered(1))
    return pl.pallas_call(
        functools.partial(_mix_ffn_kernel, alpha=alpha),
        grid=(t // tm,),
        in_specs=[
            pl.BlockSpec((tm, SSD_WIDTH), row),
            pl.BlockSpec((tm, ATTN_V), row),
            pl.BlockSpec((tm, d), row),
            mod_row(2), mod_row(4), mod_row(3), mod_row(5),
            resident((d, d)),
            resident((d, 2 * D_FF)),
            resident((D_FF, d)),
            pl.BlockSpec((1, d), const), pl.BlockSpec((1, d), const),
            pl.BlockSpec((1, d), const), pl.BlockSpec((1, d), const),
        ],
        out_specs=pl.BlockSpec((tm, d), row),
        out_shape=jax.ShapeDtypeStruct((t, d), F32),
        compiler_params=_params("parallel"),
    )(ys, ya, x2, mod3, mod3, mod3, mod3, w_out, w_gu, w_down, n1g, n1b, n2g, n2b)


def _rope_tables(seq):
    dim = DIFF_HEAD_DIM
    inv = 1.0 / (ROPE_THETA ** (jnp.arange(0, dim, 2, dtype=F32) / dim))
    ang = jnp.arange(seq, dtype=F32)[:, None] * inv[None, :]
    cos, sin = jnp.cos(ang), jnp.sin(ang)
    return (jnp.concatenate([cos, cos, cos, cos], -1), jnp.concatenate([-sin, sin, -sin, sin], -1))


def _pack_w_in(w):
    w_a = w[..., 0:ZX_W].astype(BF16)
    w_b = jnp.pad(w[..., ZX_W:], ((0, 0), (0, 0), (DT_LANE0, 0))).astype(BF16)
    return w_a, w_b


def _dt_lanes(v):
    return jnp.pad(v, (DT_LANE0, 0)).reshape(1, LANES)


def kernel(x, c, w_mod, b_mod, w_in, conv_w, conv_b, dt_bias, a_log, d_skip, ssd_norm_w, lam_qk,
           attn_norm_w, w_out, ln1_g, ln1_b, w_gate_up, w_down, ln2_g, ln2_b):
    batch, seq, d = x.shape
    depth = w_mod.shape[0]
    assert d == D_MODEL and seq % ROW_TILE == 0 and seq % ATTN_Q_TILE == 0 and batch <= SUBLANES
    t = batch * seq
    alpha = (2 * depth) ** 0.25

    cos2, sin2 = _rope_tables(seq)
    c_pad = jnp.pad(c, ((0, SUBLANES - batch), (0, 0)))
    mod = _modulation(c_pad, w_mod, b_mod)
    w_in_a, w_in_b = _pack_w_in(w_in)

    x2 = x.reshape(t, d)
    for l in range(depth):
        lambda_init = 0.8 - 0.6 * math.exp(-0.3 * l)
        mod3 = mod[l, :batch].reshape(batch * N_MOD, 1, d)
        y_ssd, q, k, vt = _inproj_ssd(
            x2, mod3, w_in_a[l], w_in_b[l], cos2, sin2,
            jnp.pad(conv_w[l], ((0, SUBLANES - CONV_WIDTH), (0, 0))),
            conv_b[l].reshape(1, SSD_XBC), _dt_lanes(dt_bias[l]), _dt_lanes(a_log[l]),
            jnp.repeat(d_skip[l], SSD_HEADDIM).reshape(1, SSD_WIDTH),
            ssd_norm_w[l].reshape(1, SSD_WIDTH), seq)
        y_attn = _attention(q, k, vt, lam_qk[l], attn_norm_w[l].reshape(DIFF_V_DIM, 1),
                            lambda_init, batch, seq)
        x2 = _mix_ffn(y_ssd, y_attn, x2, mod3, w_out[l].astype(BF16),
                      w_gate_up[l].astype(BF16), w_down[l].astype(BF16),
                      ln1_g[l].reshape(1, d), ln1_b[l].reshape(1, d),
                      ln2_g[l].reshape(1, d), ln2_b[l].reshape(1, d), alpha, seq)
    return x2.reshape(batch, seq, d)
```

```python
import functools
import math

import jax
import jax.numpy as jnp
from jax import lax
from jax.experimental import pallas as pl
from jax.experimental.pallas import tpu as pltpu

F32 = jnp.float32
BF16 = jnp.bfloat16

D_MODEL = 1024
SSD_WIDTH = 512
SSD_HEADDIM = 64
SSD_HEADS = 8
SSD_GROUPS = 2
SSD_HEADS_PER_GROUP = 4
SSD_STATE = 128
SSD_XBC = SSD_WIDTH + 2 * SSD_GROUPS * SSD_STATE
CONV_WIDTH = 4
CHUNK = 128
DIFF_HEAD_DIM = 64
DIFF_HEADS = 4
DIFF_V_DIM = 128
ATTN_QK = 512
ATTN_V = 512
ROPE_THETA = 10000.0
D_FF = 2816
N_MOD = 6
EPS = 1e-5

LANES = 128
SUBLANES = 8
VMEM_LIMIT_BYTES = 56 * 1024 * 1024

ZX_W = SSD_WIDTH + SSD_XBC
DT_LANE0 = LANES - SSD_HEADS
Q_OFF_B = LANES
K_OFF_B = Q_OFF_B + ATTN_QK
V_OFF_B = K_OFF_B + ATTN_QK
WB_W = V_OFF_B + ATTN_V

BF16_SUBLANES = 16
VT_ROWS = DIFF_V_DIM + BF16_SUBLANES
LOG2E = math.log2(math.e)

ROW_TILE = 512
ATTN_K_TILE = 512
ATTN_Q_TILE = 2 * ATTN_K_TILE
FF_TILE = 1408
FFN_ROW_TILE = 512
FFN_ROW_SPLIT = 2


def _sigmoid(x):
    return 1.0 / (1.0 + jnp.exp(-x))


def _silu(x):
    hx = 0.5 * x
    return hx + hx * jnp.tanh(hx)


def _params(*sem):
    return pltpu.CompilerParams(dimension_semantics=sem, vmem_limit_bytes=VMEM_LIMIT_BYTES)


def _mod_kernel(c_ref, w_ref, b_ref, o_ref):
    c = c_ref[...]
    cond = (c * _sigmoid(c)).astype(BF16)
    o_ref[0] = jnp.dot(cond, w_ref[0].astype(BF16), preferred_element_type=F32) + b_ref[0]


def _modulation(c_pad, w_mod, b_mod):
    depth, d, n = w_mod.shape
    tn = 2048
    return pl.pallas_call(
        _mod_kernel,
        grid=(depth, n // tn),
        in_specs=[
            pl.BlockSpec((SUBLANES, d), lambda l, j: (0, 0)),
            pl.BlockSpec((1, d, tn), lambda l, j: (l, 0, j)),
            pl.BlockSpec((1, 1, tn), lambda l, j: (l, 0, j)),
        ],
        out_specs=pl.BlockSpec((1, SUBLANES, tn), lambda l, j: (l, 0, j)),
        out_shape=jax.ShapeDtypeStruct((depth, SUBLANES, n), F32),
        compiler_params=_params("parallel", "parallel"),
    )(c_pad, w_mod, b_mod.reshape(depth, 1, n))


def _split(x, pieces):
    out = []
    for _ in range(pieces):
        p = x.astype(BF16)
        out.append(p)
        x = x - p.astype(F32)
    return out


def _ssd_chunk(u, z, dt_raw, cw_ref, cb_ref, dtb_ref, alog_ref, dsk_ref, nw_ref, ex_ref, xpad, state):
    L = CHUNK
    dot = functools.partial(jnp.dot, preferred_element_type=F32)

    def taps(v):
        acc = cb_ref[...] + cw_ref[CONV_WIDTH - 1:CONV_WIDTH, :] * v
        for kk in range(CONV_WIDTH - 1):
            acc = acc + cw_ref[kk:kk + 1, :] * pltpu.roll(v, CONV_WIDTH - 1 - kk, 0)
        return acc

    head = taps(jnp.concatenate([xpad[...], u[0:SUBLANES]], axis=0))[SUBLANES:]
    conv = jnp.concatenate([head, taps(u)[SUBLANES:]], axis=0)
    xpad[...] = u[L - SUBLANES:L]
    xc = _silu(conv)
    xs = xc[:, 0:SSD_WIDTH]

    pre = dt_raw + dtb_ref[...]
    dt = jnp.maximum(pre, 0.0) + jnp.log(1.0 + jnp.exp(-jnp.abs(pre)))
    a = dt * (-jnp.exp(alog_ref[...]))
    row = lax.broadcasted_iota(jnp.int32, (L, L), 0)
    col = lax.broadcasted_iota(jnp.int32, (L, L), 1)
    causal = row >= col
    tril = causal.astype(BF16)
    cs = sum(dot(tril, p) for p in _split(a, 3))
    cs_t = cs.T
    expand = lambda v: sum(dot(p, ex_ref[...]) for p in _split(v, 2))
    xdt = xs * expand(dt)
    ecs_x = jnp.exp(expand(cs))
    even_head = (lax.broadcasted_iota(jnp.int32, xdt.shape, 1) // SSD_HEADDIM) % 2 == 0
    xdt_even = jnp.where(even_head, xdt, 0.0).astype(BF16)
    xdt_odd = jnp.where(even_head, 0.0, xdt).astype(BF16)

    ys = []
    for g in range(SSD_GROUPS):
        bm = xc[:, SSD_WIDTH + g * SSD_STATE:SSD_WIDTH + (g + 1) * SSD_STATE]
        cm = xc[:, SSD_WIDTH + (SSD_GROUPS + g) * SSD_STATE:SSD_WIDTH + (SSD_GROUPS + g + 1) * SSD_STATE]
        cm_b = cm.astype(BF16)
        cb = lax.dot_general(cm_b, bm.astype(BF16), (((1,), (1,)), ((), ())),
                             preferred_element_type=F32)
        bm_t = bm.T
        for pp in range(SSD_HEADS_PER_GROUP // 2):
            pr = g * (SSD_HEADS_PER_GROUP // 2) + pp
            halves = (xdt_even[:, pr * LANES:(pr + 1) * LANES],
                      xdt_odd[:, pr * LANES:(pr + 1) * LANES])
            y_pair = None
            st_pair = None
            for half in range(2):
                hd = DT_LANE0 + 2 * pr + half
                cs_col = cs[:, hd:hd + 1]
                cs_row = cs_t[hd:hd + 1, :]
                cs_last = cs_t[hd:hd + 1, L - 1:L]
                lmat = jnp.exp(jnp.where(causal, cs_col - cs_row, -jnp.inf))
                y_h = jnp.dot((cb * lmat).astype(BF16), halves[half], preferred_element_type=F32)
                decay = jnp.exp(cs_last - cs_row)
                st_h = jnp.dot((bm_t * decay).astype(BF16), halves[half],
                               preferred_element_type=F32)
                y_pair = y_h if y_pair is None else y_pair + y_h
                st_pair = st_h if st_pair is None else st_pair + st_h
            prev = state[pr]
            ecs_p = ecs_x[:, pr * LANES:(pr + 1) * LANES]
            y_off = jnp.dot(cm_b, prev.astype(BF16), preferred_element_type=F32) * ecs_p
            state[pr] = ecs_p[L - 1:L, :] * prev + st_pair
            ys.append(y_pair + y_off)

    y = (jnp.concatenate(ys, axis=1) + dsk_ref[...] * xs) * _silu(z)
    gw = SSD_WIDTH // SSD_GROUPS
    out = []
    for g in range(SSD_GROUPS):
        yg = y[:, g * gw:(g + 1) * gw]
        ms = jnp.mean(yg * yg, axis=-1, keepdims=True)
        out.append((yg * lax.rsqrt(ms + EPS) * nw_ref[:, g * gw:(g + 1) * gw]).astype(BF16))
    return jnp.concatenate(out, axis=1)


def _inproj_ssd_kernel(x_ref, sc_ref, sh_ref, wa_ref, wb_ref, cos_ref, sin_ref,
                       cw_ref, cb_ref, dtb_ref, alog_ref, dsk_ref, nw_ref, ex_ref,
                       y_ref, qt_ref, k_ref, vt_ref, xpad, state, *, per_seq):
    @pl.when(pl.program_id(0) % per_seq == 0)
    def _():
        xpad[...] = jnp.zeros_like(xpad)
        state[...] = jnp.zeros_like(state)

    dot = functools.partial(jnp.dot, preferred_element_type=F32)
    h = (x_ref[...] * (1.0 + sc_ref[0]) + sh_ref[0]).astype(BF16)
    cos = cos_ref[...]
    sin = sin_ref[...]
    lane = lax.broadcasted_iota(jnp.int32, cos.shape, 1)
    first_half = (lane % DIFF_HEAD_DIM) < (DIFF_HEAD_DIM // 2)

    def rope(t):
        rot = jnp.where(first_half, pltpu.roll(t, LANES - DIFF_HEAD_DIM // 2, 1),
                        pltpu.roll(t, DIFF_HEAD_DIM // 2, 1))
        return t * cos + rot * sin

    ssd = functools.partial(_ssd_chunk, cw_ref=cw_ref, cb_ref=cb_ref, dtb_ref=dtb_ref,
                            alog_ref=alog_ref, dsk_ref=dsk_ref, nw_ref=nw_ref, ex_ref=ex_ref,
                            xpad=xpad, state=state)

    def ssd_rows(c):
        rows = slice(c * CHUNK, (c + 1) * CHUNK)
        y_ref[rows, :] = ssd(xbc[rows], z[rows], dtq[rows, 0:LANES])

    xbc = dot(h, wa_ref[:, SSD_WIDTH:ZX_W])
    dtq = dot(h, wb_ref[:, 0:K_OFF_B])
    z = dot(h, wa_ref[:, 0:SSD_WIDTH])
    ssd_rows(0)
    k = dot(h, wb_ref[:, K_OFF_B:V_OFF_B])
    ssd_rows(1)
    v = dot(h, wb_ref[:, V_OFF_B:WB_W])
    ssd_rows(2)
    ones = jnp.ones((BF16_SUBLANES, h.shape[0]), BF16)
    for hd in range(DIFF_HEADS):
        lo = hd * LANES
        qh = rope(dtq[:, Q_OFF_B + lo:Q_OFF_B + lo + LANES]) * (LOG2E * DIFF_HEAD_DIM ** -0.5)
        k_ref[:, lo:lo + LANES] = rope(k[:, lo:lo + LANES]).astype(BF16)
        qt_ref[lo:lo + LANES, :] = qh.T.astype(BF16)
        vt_ref[hd * VT_ROWS:hd * VT_ROWS + DIFF_V_DIM, :] = v[:, lo:lo + LANES].T.astype(BF16)
        vt_ref[hd * VT_ROWS + DIFF_V_DIM:(hd + 1) * VT_ROWS, :] = ones
    ssd_rows(3)


def _inproj_ssd(x2, mod3, layer, w_a, w_b, cos2, sin2, conv_w8, conv_b, dtb, alog, dsk, nw, seq):
    t, d = x2.shape
    tm = ROW_TILE
    assert tm == 4 * CHUNK
    per_seq = seq // tm
    row = lambda i: (i, 0)
    const = lambda i: (0, 0)
    head_of_col = jnp.arange(SSD_WIDTH)[None, :] // SSD_HEADDIM
    head = jnp.arange(LANES)[:, None] - DT_LANE0
    expand = (head_of_col == head).astype(BF16)
    return pl.pallas_call(
        functools.partial(_inproj_ssd_kernel, per_seq=per_seq),
        grid=(t // tm,),
        in_specs=[
            pl.BlockSpec((tm, d), row),
            pl.BlockSpec((1, 1, d), lambda i: ((i // per_seq) * N_MOD + 1, 0, 0)),
            pl.BlockSpec((1, 1, d), lambda i: ((i // per_seq) * N_MOD + 0, 0, 0)),
            pl.BlockSpec((None, d, ZX_W), lambda i: (layer, 0, 0)),
            pl.BlockSpec((None, d, WB_W), lambda i: (layer, 0, 0)),
            pl.BlockSpec((tm, LANES), lambda i: (i % per_seq, 0)),
            pl.BlockSpec((tm, LANES), lambda i: (i % per_seq, 0)),
            pl.BlockSpec((SUBLANES, SSD_XBC), const),
            pl.BlockSpec((1, SSD_XBC), const),
            pl.BlockSpec((1, LANES), const),
            pl.BlockSpec((1, LANES), const),
            pl.BlockSpec((1, SSD_WIDTH), const),
            pl.BlockSpec((1, SSD_WIDTH), const),
            pl.BlockSpec((LANES, SSD_WIDTH), const),
        ],
        out_specs=[
            pl.BlockSpec((tm, SSD_WIDTH), row),
            pl.BlockSpec((ATTN_QK, tm), lambda i: (0, i)),
            pl.BlockSpec((tm, ATTN_QK), row),
            pl.BlockSpec((DIFF_HEADS * VT_ROWS, tm), lambda i: (0, i)),
        ],
        out_shape=[
            jax.ShapeDtypeStruct((t, SSD_WIDTH), BF16),
            jax.ShapeDtypeStruct((ATTN_QK, t), BF16),
            jax.ShapeDtypeStruct((t, ATTN_QK), BF16),
            jax.ShapeDtypeStruct((DIFF_HEADS * VT_ROWS, t), BF16),
        ],
        scratch_shapes=[
            pltpu.VMEM((SUBLANES, SSD_XBC), F32),
            pltpu.VMEM((SSD_HEADS // 2, SSD_STATE, 2 * SSD_HEADDIM), F32),
        ],
        compiler_params=_params("arbitrary"),
    )(x2, mod3, mod3, w_a, w_b, cos2, sin2, conv_w8, conv_b, dtb, alog, dsk, nw, expand)


def _attn_kernel(qt_ref, k_ref, vt_ref, lam_ref, nw_ref, o_ref, sa_ref, sb_ref, mxa_ref, mxb_ref,
                 acc_ref, m_ref, *, lambda_init, nq):
    tq, tk = ATTN_Q_TILE, ATTN_K_TILE
    dv = DIFF_V_DIM
    buf_a = (sa_ref, mxa_ref)
    buf_b = (sb_ref, mxb_ref)
    lq = lam_ref[...]
    lam = (jnp.exp(jnp.sum(lq[0:1] * lq[1:2], axis=-1, keepdims=True))
           - jnp.exp(jnp.sum(lq[2:3] * lq[3:4], axis=-1, keepdims=True)) + lambda_init)

    def finalize(blk):
        r0 = 1.0 / acc_ref[0, dv:dv + 1, :]
        r1 = lam / acc_ref[1, dv:dv + 1, :]
        o = acc_ref[0, 0:dv, :] * r0 - acc_ref[1, 0:dv, :] * r1
        ms = jnp.mean(o * o, axis=0, keepdims=True)
        o = o * lax.rsqrt(ms + EPS) * nw_ref[...] * (1.0 - lambda_init)
        o_ref[pl.ds(pl.multiple_of(blk * tq, tq), tq), :] = o.T.astype(BF16)

    halves = ((0, tk), (tk, tq))
    quarters = tuple((c * (tk // 2), (c + 1) * (tk // 2)) for c in range(4))
    pieces = tuple((m, lo, hi) for m in range(2) for lo, hi in quarters)

    def scores(j, buf, qmaps, piece):
        s_ref, mx_ref = buf
        m, lo, hi = piece
        kt = k_ref[pl.ds(pl.multiple_of(j * tk, tk), tk), :]
        s = jnp.dot(kt, qmaps[m][:, lo:hi], preferred_element_type=F32)
        s_ref[m, :, lo:hi] = s
        mx_ref[m, :, lo:hi] = jnp.max(s, axis=0, keepdims=True)

    def consume(j, buf, piece, diagonal=False):
        s_ref, mx_ref = buf
        m, lo, hi = piece
        vt = vt_ref[:, pl.ds(pl.multiple_of(j * tk, tk), tk)]
        s = s_ref[m, :, lo:hi]
        mx = mx_ref[m, :, lo:hi]
        m_old = m_ref[m, :, lo:hi]
        if diagonal:
            krow = lax.broadcasted_iota(jnp.int32, s.shape, 0)
            qcol = lax.broadcasted_iota(jnp.int32, s.shape, 1)
            s = jnp.where(krow <= qcol, s, -jnp.inf)
            mx = jnp.max(s, axis=0, keepdims=True)
        m_new = jnp.maximum(m_old, mx)
        p = jnp.exp2(s - m_new).astype(BF16)
        acc_ref[m, :, lo:hi] = (jnp.exp2(m_old - m_new) * acc_ref[m, :, lo:hi]
                                + jnp.dot(vt, p, preferred_element_type=F32))
        m_ref[m, :, lo:hi] = m_new

    acc_ref[...] = jnp.ones_like(acc_ref)

    @pl.loop(0, nq)
    def _(qi):
        qt = qt_ref[:, pl.ds(pl.multiple_of(qi * tq, tq), tq)]
        feat = lax.broadcasted_iota(jnp.int32, qt.shape, 0)
        zero = jnp.zeros_like(qt)
        qmaps = (jnp.where(feat < DIFF_HEAD_DIM, qt, zero), jnp.where(feat >= DIFF_HEAD_DIM, qt, zero))

        finalize(jnp.maximum(qi - 1, 0))
        for pc in pieces:
            scores(0, buf_a, qmaps, pc)
        acc_ref[...] = jnp.zeros_like(acc_ref)
        m_ref[...] = jnp.full(m_ref.shape, -jnp.inf, F32)

        @pl.loop(0, qi)
        def _(jj):
            j = 2 * jj
            for pc in pieces:
                scores(j + 1, buf_b, qmaps, pc)
                consume(j, buf_a, pc)
            for pc in pieces:
                scores(j + 2, buf_a, qmaps, pc)
                consume(j + 1, buf_b, pc)

        for m in range(2):
            scores(2 * qi + 1, buf_b, qmaps, (m,) + halves[1])
            consume(2 * qi, buf_a, (m,) + halves[0], diagonal=True)
            consume(2 * qi, buf_a, (m,) + halves[1])
        for m in range(2):
            consume(2 * qi + 1, buf_b, (m,) + halves[1], diagonal=True)

    finalize(nq - 1)


def _attention(qt, k, vt, lam_qk, nw_col, lambda_init, batch, seq):
    t = k.shape[0]
    tq, tk = ATTN_Q_TILE, ATTN_K_TILE
    nq = seq // tq
    return pl.pallas_call(
        functools.partial(_attn_kernel, lambda_init=lambda_init, nq=nq),
        grid=(batch, DIFF_HEADS),
        in_specs=[
            pl.BlockSpec((LANES, seq), lambda b, h: (h, b)),
            pl.BlockSpec((seq, LANES), lambda b, h: (b, h)),
            pl.BlockSpec((VT_ROWS, seq), lambda b, h: (h, b)),
            pl.BlockSpec((4, DIFF_HEAD_DIM), lambda b, h: (0, 0)),
            pl.BlockSpec((DIFF_V_DIM, 1), lambda b, h: (0, 0)),
        ],
        out_specs=pl.BlockSpec((seq, LANES), lambda b, h: (b, h)),
        out_shape=jax.ShapeDtypeStruct((t, ATTN_V), BF16),
        scratch_shapes=[pltpu.VMEM((2, tk, tq), F32), pltpu.VMEM((2, tk, tq), F32),
                        pltpu.VMEM((2, 1, tq), F32), pltpu.VMEM((2, 1, tq), F32),
                        pltpu.VMEM((2, VT_ROWS, tq), F32), pltpu.VMEM((2, 1, tq), F32)],
        compiler_params=_params("parallel", "parallel"),
    )(qt, k, vt, lam_qk, nw_col)


def _layer_norm(r, g, b):
    mu = jnp.mean(r, axis=-1, keepdims=True)
    d = r - mu
    var = jnp.mean(d * d, axis=-1, keepdims=True)
    return d * lax.rsqrt(var + EPS) * g + b


def _mix_ffn_kernel(ys_ref, ya_ref, x_ref, g1_ref, sc2_ref, sh2_ref, g2_ref, wo_ref, wgu_ref, wd_ref,
                    n1g_ref, n1b_ref, n2g_ref, n2b_ref, o_ref, *, alpha):
    tm = x_ref.shape[0]
    halves = [pl.ds(s * (tm // FFN_ROW_SPLIT), tm // FFN_ROW_SPLIT) for s in range(FFN_ROW_SPLIT)]
    dot = functools.partial(jnp.dot, preferred_element_type=F32)

    def outproj(rows):
        return (dot(ys_ref[rows, :], wo_ref[0:SSD_WIDTH, :])
                + dot(ya_ref[rows, :], wo_ref[SSD_WIDTH:D_MODEL, :]))

    def norm_modulate(y, rows):
        x1 = _layer_norm(alpha * x_ref[rows, :] + (1.0 + g1_ref[0]) * y, n1g_ref[...], n1b_ref[...])
        return x1, (x1 * (1.0 + sc2_ref[0]) + sh2_ref[0]).astype(BF16)

    def swiglu(h):
        acc = None
        for c in range(D_FF // FF_TILE):
            lo = c * FF_TILE
            gt = dot(h, wgu_ref[:, lo:lo + FF_TILE])
            up = dot(h, wgu_ref[:, D_FF + lo:D_FF + lo + FF_TILE])
            part = dot((_silu(gt) * up).astype(BF16), wd_ref[lo:lo + FF_TILE, :])
            acc = part if acc is None else acc + part
        return acc

    ys = [outproj(rows) for rows in halves]
    outs = []
    for y, rows in zip(ys, halves):
        x1, h = norm_modulate(y, rows)
        outs.append((x1, swiglu(h)))
    for (x1, acc), rows in zip(outs, halves):
        o_ref[rows, :] = _layer_norm(alpha * x1 + (1.0 + g2_ref[0]) * acc, n2g_ref[...], n2b_ref[...])


def _mix_ffn(ys, ya, x2, mod3, layer, w_out, w_gu, w_down, n1g, n1b, n2g, n2b, alpha, seq):
    t, d = x2.shape
    tm = FFN_ROW_TILE
    per_seq = seq // tm
    row = lambda i: (i, 0)
    const = lambda i: (0, 0)
    mod_row = lambda k: pl.BlockSpec((1, 1, d), lambda i: ((i // per_seq) * N_MOD + k, 0, 0))
    resident = lambda shape: pl.BlockSpec((None,) + shape, lambda i: (layer, 0, 0),
                                          pipeline_mode=pl.Buffered(1))
    return pl.pallas_call(
        functools.partial(_mix_ffn_kernel, alpha=alpha),
        grid=(t // tm,),
        in_specs=[
            pl.BlockSpec((tm, SSD_WIDTH), row),
            pl.BlockSpec((tm, ATTN_V), row),
            pl.BlockSpec((tm, d), row),
            mod_row(2), mod_row(4), mod_row(3), mod_row(5),
            resident((d, d)),
            resident((d, 2 * D_FF)),
            resident((D_FF, d)),
            pl.BlockSpec((1, d), const), pl.BlockSpec((1, d), const),
            pl.BlockSpec((1, d), const), pl.BlockSpec((1, d), const),
        ],
        out_specs=pl.BlockSpec((tm, d), row),
        out_shape=jax.ShapeDtypeStruct((t, d), F32),
        compiler_params=_params("parallel"),
    )(ys, ya, x2, mod3, mod3, mod3, mod3, w_out, w_gu, w_down, n1g, n1b, n2g, n2b)


def _rope_tables(seq):
    dim = DIFF_HEAD_DIM
    inv = 1.0 / (ROPE_THETA ** (jnp.arange(0, dim, 2, dtype=F32) / dim))
    ang = jnp.arange(seq, dtype=F32)[:, None] * inv[None, :]
    cos, sin = jnp.cos(ang), jnp.sin(ang)
    return (jnp.concatenate([cos, cos, cos, cos], -1), jnp.concatenate([-sin, sin, -sin, sin], -1))


def _pack_w_in(w):
    w_a = w[..., 0:ZX_W].astype(BF16)
    w_b = jnp.pad(w[..., ZX_W:], ((0, 0), (0, 0), (DT_LANE0, 0))).astype(BF16)
    return w_a, w_b


def _dt_lanes(v):
    return jnp.pad(v, (DT_LANE0, 0)).reshape(1, LANES)


def kernel(x, c, w_mod, b_mod, w_in, conv_w, conv_b, dt_bias, a_log, d_skip, ssd_norm_w, lam_qk,
           attn_norm_w, w_out, ln1_g, ln1_b, w_gate_up, w_down, ln2_g, ln2_b):
    batch, seq, d = x.shape
    depth = w_mod.shape[0]
    assert d == D_MODEL and seq % ROW_TILE == 0 and seq % ATTN_Q_TILE == 0 and batch <= SUBLANES
    t = batch * seq
    alpha = (2 * depth) ** 0.25

    cos2, sin2 = _rope_tables(seq)
    c_pad = jnp.pad(c, ((0, SUBLANES - batch), (0, 0)))
    mod = _modulation(c_pad, w_mod, b_mod)
    w_in_a, w_in_b = _pack_w_in(w_in)
    w_out_b, w_gu_b, w_down_b = w_out.astype(BF16), w_gate_up.astype(BF16), w_down.astype(BF16)

    x2 = x.reshape(t, d)
    for l in range(depth):
        lambda_init = 0.8 - 0.6 * math.exp(-0.3 * l)
        mod3 = mod[l, :batch].reshape(batch * N_MOD, 1, d)
        y_ssd, q, k, vt = _inproj_ssd(
            x2, mod3, l, w_in_a, w_in_b, cos2, sin2,
            jnp.pad(conv_w[l], ((0, SUBLANES - CONV_WIDTH), (0, 0))),
            conv_b[l].reshape(1, SSD_XBC), _dt_lanes(dt_bias[l]), _dt_lanes(a_log[l]),
            jnp.repeat(d_skip[l], SSD_HEADDIM).reshape(1, SSD_WIDTH),
            ssd_norm_w[l].reshape(1, SSD_WIDTH), seq)
        y_attn = _attention(q, k, vt, lam_qk[l], attn_norm_w[l].reshape(DIFF_V_DIM, 1),
                            lambda_init, batch, seq)
        x2 = _mix_ffn(y_ssd, y_attn, x2, mod3, l, w_out_b, w_gu_b, w_down_b,
                      ln1_g[l].reshape(1, d), ln1_b[l].reshape(1, d),
                      ln2_g[l].reshape(1, d), ln2_b[l].reshape(1, d), alpha, seq)
    return x2.reshape(batch, seq, d)
```

```python
import functools
import math

import jax
import jax.numpy as jnp
from jax import lax
from jax.experimental import pallas as pl
from jax.experimental.pallas import tpu as pltpu

F32 = jnp.float32
BF16 = jnp.bfloat16

D_MODEL = 1024
SSD_WIDTH = 512
SSD_HEADDIM = 64
SSD_HEADS = 8
SSD_GROUPS = 2
SSD_HEADS_PER_GROUP = 4
SSD_STATE = 128
SSD_XBC = SSD_WIDTH + 2 * SSD_GROUPS * SSD_STATE
CONV_WIDTH = 4
CHUNK = 128
DIFF_HEAD_DIM = 64
DIFF_HEADS = 4
DIFF_V_DIM = 128
ATTN_QK = 512
ATTN_V = 512
ROPE_THETA = 10000.0
D_FF = 2816
N_MOD = 6
EPS = 1e-5

LANES = 128
SUBLANES = 8
VMEM_LIMIT_BYTES = 56 * 1024 * 1024

ZX_W = SSD_WIDTH + SSD_XBC
DT_LANE0 = LANES - SSD_HEADS
Q_OFF_B = LANES
K_OFF_B = Q_OFF_B + ATTN_QK
V_OFF_B = K_OFF_B + ATTN_QK
WB_W = V_OFF_B + ATTN_V

BF16_SUBLANES = 16
VT_ROWS = DIFF_V_DIM + BF16_SUBLANES
LOG2E = math.log2(math.e)

ROW_TILE = 512
ATTN_K_TILE = 512
ATTN_Q_TILE = 2 * ATTN_K_TILE
FF_TILE = 1408
FFN_ROW_TILE = 512
FFN_ROW_SPLIT = 2


def _sigmoid(x):
    return 1.0 / (1.0 + jnp.exp(-x))


def _silu(x):
    hx = 0.5 * x
    return hx + hx * jnp.tanh(hx)


def _params(*sem):
    return pltpu.CompilerParams(dimension_semantics=sem, vmem_limit_bytes=VMEM_LIMIT_BYTES)


def _mod_kernel(c_ref, w_ref, b_ref, o_ref):
    c = c_ref[...]
    cond = (c * _sigmoid(c)).astype(BF16)
    o_ref[0] = jnp.dot(cond, w_ref[0].astype(BF16), preferred_element_type=F32) + b_ref[0]


def _modulation(c_pad, w_mod, b_mod):
    depth, d, n = w_mod.shape
    tn = 2048
    return pl.pallas_call(
        _mod_kernel,
        grid=(depth, n // tn),
        in_specs=[
            pl.BlockSpec((SUBLANES, d), lambda l, j: (0, 0)),
            pl.BlockSpec((1, d, tn), lambda l, j: (l, 0, j)),
            pl.BlockSpec((1, 1, tn), lambda l, j: (l, 0, j)),
        ],
        out_specs=pl.BlockSpec((1, SUBLANES, tn), lambda l, j: (l, 0, j)),
        out_shape=jax.ShapeDtypeStruct((depth, SUBLANES, n), F32),
        compiler_params=_params("parallel", "parallel"),
    )(c_pad, w_mod, b_mod.reshape(depth, 1, n))


def _split(x, pieces):
    out = []
    for _ in range(pieces):
        p = x.astype(BF16)
        out.append(p)
        x = x - p.astype(F32)
    return out


def _ssd_chunk(u, z, dt_raw, cw_ref, cb_ref, dtb_ref, alog_ref, dsk_ref, nw_ref, ex_ref, xpad, state):
    L = CHUNK
    dot = functools.partial(jnp.dot, preferred_element_type=F32)

    def taps(v):
        acc = cb_ref[...] + cw_ref[CONV_WIDTH - 1:CONV_WIDTH, :] * v
        for kk in range(CONV_WIDTH - 1):
            acc = acc + cw_ref[kk:kk + 1, :] * pltpu.roll(v, CONV_WIDTH - 1 - kk, 0)
        return acc

    head = taps(jnp.concatenate([xpad[...], u[0:SUBLANES]], axis=0))[SUBLANES:]
    conv = jnp.concatenate([head, taps(u)[SUBLANES:]], axis=0)
    xpad[...] = u[L - SUBLANES:L]
    xc = _silu(conv)
    xs = xc[:, 0:SSD_WIDTH]

    pre = dt_raw + dtb_ref[...]
    dt = jnp.maximum(pre, 0.0) + jnp.log(1.0 + jnp.exp(-jnp.abs(pre)))
    a = dt * (-jnp.exp(alog_ref[...]))
    row = lax.broadcasted_iota(jnp.int32, (L, L), 0)
    col = lax.broadcasted_iota(jnp.int32, (L, L), 1)
    causal = row >= col
    tril = causal.astype(BF16)
    cs = sum(dot(tril, p) for p in _split(a, 3))
    cs_t = cs.T
    expand = lambda v: sum(dot(p, ex_ref[...]) for p in _split(v, 2))
    xdt = xs * expand(dt)
    ecs_x = jnp.exp(expand(cs))
    even_head = (lax.broadcasted_iota(jnp.int32, xdt.shape, 1) // SSD_HEADDIM) % 2 == 0
    xdt_even = jnp.where(even_head, xdt, 0.0).astype(BF16)
    xdt_odd = jnp.where(even_head, 0.0, xdt).astype(BF16)

    ys = []
    for g in range(SSD_GROUPS):
        bm = xc[:, SSD_WIDTH + g * SSD_STATE:SSD_WIDTH + (g + 1) * SSD_STATE]
        cm = xc[:, SSD_WIDTH + (SSD_GROUPS + g) * SSD_STATE:SSD_WIDTH + (SSD_GROUPS + g + 1) * SSD_STATE]
        cm_b = cm.astype(BF16)
        cb = lax.dot_general(cm_b, bm.astype(BF16), (((1,), (1,)), ((), ())),
                             preferred_element_type=F32)
        bm_t = bm.T
        for pp in range(SSD_HEADS_PER_GROUP // 2):
            pr = g * (SSD_HEADS_PER_GROUP // 2) + pp
            halves = (xdt_even[:, pr * LANES:(pr + 1) * LANES],
                      xdt_odd[:, pr * LANES:(pr + 1) * LANES])
            y_pair = None
            st_pair = None
            for half in range(2):
                hd = DT_LANE0 + 2 * pr + half
                cs_col = cs[:, hd:hd + 1]
                cs_row = cs_t[hd:hd + 1, :]
                cs_last = cs_t[hd:hd + 1, L - 1:L]
                lmat = jnp.exp(jnp.where(causal, cs_col - cs_row, -jnp.inf))
                y_h = jnp.dot((cb * lmat).astype(BF16), halves[half], preferred_element_type=F32)
                decay = jnp.exp(cs_last - cs_row)
                st_h = jnp.dot((bm_t * decay).astype(BF16), halves[half],
                               preferred_element_type=F32)
                y_pair = y_h if y_pair is None else y_pair + y_h
                st_pair = st_h if st_pair is None else st_pair + st_h
            prev = state[pr]
            ecs_p = ecs_x[:, pr * LANES:(pr + 1) * LANES]
            y_off = jnp.dot(cm_b, prev.astype(BF16), preferred_element_type=F32) * ecs_p
            state[pr] = ecs_p[L - 1:L, :] * prev + st_pair
            ys.append(y_pair + y_off)

    y = (jnp.concatenate(ys, axis=1) + dsk_ref[...] * xs) * _silu(z)
    gw = SSD_WIDTH // SSD_GROUPS
    out = []
    for g in range(SSD_GROUPS):
        yg = y[:, g * gw:(g + 1) * gw]
        ms = jnp.mean(yg * yg, axis=-1, keepdims=True)
        out.append((yg * lax.rsqrt(ms + EPS) * nw_ref[:, g * gw:(g + 1) * gw]).astype(BF16))
    return jnp.concatenate(out, axis=1)


def _inproj_ssd_kernel(x_ref, sc_ref, sh_ref, wa_ref, wb_ref, cos_ref, sin_ref,
                       cw_ref, cb_ref, dtb_ref, alog_ref, dsk_ref, nw_ref, ex_ref,
                       y_ref, qt_ref, k_ref, vt_ref, xpad, state, *, per_seq):
    @pl.when(pl.program_id(0) % per_seq == 0)
    def _():
        xpad[...] = jnp.zeros_like(xpad)
        state[...] = jnp.zeros_like(state)

    dot = functools.partial(jnp.dot, preferred_element_type=F32)
    h = (x_ref[...] * (1.0 + sc_ref[0]) + sh_ref[0]).astype(BF16)
    cos = cos_ref[...]
    sin = sin_ref[...]
    lane = lax.broadcasted_iota(jnp.int32, cos.shape, 1)
    first_half = (lane % DIFF_HEAD_DIM) < (DIFF_HEAD_DIM // 2)

    def rope(t):
        rot = jnp.where(first_half, pltpu.roll(t, LANES - DIFF_HEAD_DIM // 2, 1),
                        pltpu.roll(t, DIFF_HEAD_DIM // 2, 1))
        return t * cos + rot * sin

    ssd = functools.partial(_ssd_chunk, cw_ref=cw_ref, cb_ref=cb_ref, dtb_ref=dtb_ref,
                            alog_ref=alog_ref, dsk_ref=dsk_ref, nw_ref=nw_ref, ex_ref=ex_ref,
                            xpad=xpad, state=state)

    def ssd_rows(c):
        rows = slice(c * CHUNK, (c + 1) * CHUNK)
        y_ref[rows, :] = ssd(xbc[rows], z[rows], dtq[rows, 0:LANES])

    xbc = dot(h, wa_ref[:, SSD_WIDTH:ZX_W])
    dtq = dot(h, wb_ref[:, 0:K_OFF_B])
    z = dot(h, wa_ref[:, 0:SSD_WIDTH])
    ssd_rows(0)
    k = dot(h, wb_ref[:, K_OFF_B:V_OFF_B])
    ssd_rows(1)
    v = dot(h, wb_ref[:, V_OFF_B:WB_W])
    ssd_rows(2)
    ones = jnp.ones((BF16_SUBLANES, h.shape[0]), BF16)
    for hd in range(DIFF_HEADS):
        lo = hd * LANES
        qh = rope(dtq[:, Q_OFF_B + lo:Q_OFF_B + lo + LANES]) * (LOG2E * DIFF_HEAD_DIM ** -0.5)
        k_ref[:, lo:lo + LANES] = rope(k[:, lo:lo + LANES]).astype(BF16)
        qt_ref[lo:lo + LANES, :] = qh.T.astype(BF16)
        vt_ref[hd * VT_ROWS:hd * VT_ROWS + DIFF_V_DIM, :] = v[:, lo:lo + LANES].T.astype(BF16)
        vt_ref[hd * VT_ROWS + DIFF_V_DIM:(hd + 1) * VT_ROWS, :] = ones
    ssd_rows(3)


def _inproj_ssd(x2, mod3, layer, w_a, w_b, cos2, sin2, conv_w8, conv_b, dtb, alog, dsk, nw, seq):
    t, d = x2.shape
    tm = ROW_TILE
    assert tm == 4 * CHUNK
    per_seq = seq // tm
    row = lambda i: (i, 0)
    const = lambda i: (0, 0)
    head_of_col = jnp.arange(SSD_WIDTH)[None, :] // SSD_HEADDIM
    head = jnp.arange(LANES)[:, None] - DT_LANE0
    expand = (head_of_col == head).astype(BF16)
    return pl.pallas_call(
        functools.partial(_inproj_ssd_kernel, per_seq=per_seq),
        grid=(t // tm,),
        in_specs=[
            pl.BlockSpec((tm, d), row),
            pl.BlockSpec((1, 1, d), lambda i: ((i // per_seq) * N_MOD + 1, 0, 0)),
            pl.BlockSpec((1, 1, d), lambda i: ((i // per_seq) * N_MOD + 0, 0, 0)),
            pl.BlockSpec((None, d, ZX_W), lambda i: (layer, 0, 0)),
            pl.BlockSpec((None, d, WB_W), lambda i: (layer, 0, 0)),
            pl.BlockSpec((tm, LANES), lambda i: (i % per_seq, 0)),
            pl.BlockSpec((tm, LANES), lambda i: (i % per_seq, 0)),
            pl.BlockSpec((SUBLANES, SSD_XBC), const),
            pl.BlockSpec((1, SSD_XBC), const),
            pl.BlockSpec((1, LANES), const),
            pl.BlockSpec((1, LANES), const),
            pl.BlockSpec((1, SSD_WIDTH), const),
            pl.BlockSpec((1, SSD_WIDTH), const),
            pl.BlockSpec((LANES, SSD_WIDTH), const),
        ],
        out_specs=[
            pl.BlockSpec((tm, SSD_WIDTH), row),
            pl.BlockSpec((ATTN_QK, tm), lambda i: (0, i)),
            pl.BlockSpec((tm, ATTN_QK), row),
            pl.BlockSpec((DIFF_HEADS * VT_ROWS, tm), lambda i: (0, i)),
        ],
        out_shape=[
            jax.ShapeDtypeStruct((t, SSD_WIDTH), BF16),
            jax.ShapeDtypeStruct((ATTN_QK, t), BF16),
            jax.ShapeDtypeStruct((t, ATTN_QK), BF16),
            jax.ShapeDtypeStruct((DIFF_HEADS * VT_ROWS, t), BF16),
        ],
        scratch_shapes=[
            pltpu.VMEM((SUBLANES, SSD_XBC), F32),
            pltpu.VMEM((SSD_HEADS // 2, SSD_STATE, 2 * SSD_HEADDIM), F32),
        ],
        compiler_params=_params("arbitrary"),
    )(x2, mod3, mod3, w_a, w_b, cos2, sin2, conv_w8, conv_b, dtb, alog, dsk, nw, expand)


def _attn_kernel(qt_ref, k_ref, vt_ref, lam_ref, nw_ref, o_ref, sa_ref, sb_ref, mxa_ref, mxb_ref,
                 acc_ref, m_ref, qm_ref, *, lambda_init, nq):
    tq, tk = ATTN_Q_TILE, ATTN_K_TILE
    dv = DIFF_V_DIM
    buf_a = (sa_ref, mxa_ref)
    buf_b = (sb_ref, mxb_ref)
    lq = lam_ref[...]
    lam = (jnp.exp(jnp.sum(lq[0:1] * lq[1:2], axis=-1, keepdims=True))
           - jnp.exp(jnp.sum(lq[2:3] * lq[3:4], axis=-1, keepdims=True)) + lambda_init)

    def finalize(blk):
        r0 = 1.0 / acc_ref[0, dv:dv + 1, :]
        r1 = lam / acc_ref[1, dv:dv + 1, :]
        o = acc_ref[0, 0:dv, :] * r0 - acc_ref[1, 0:dv, :] * r1
        ms = jnp.mean(o * o, axis=0, keepdims=True)
        o = o * lax.rsqrt(ms + EPS) * nw_ref[...] * (1.0 - lambda_init)
        o_ref[pl.ds(pl.multiple_of(blk * tq, tq), tq), :] = o.T.astype(BF16)

    quarters = tuple((c * (tk // 2), (c + 1) * (tk // 2)) for c in range(4))
    pieces = tuple((m, lo, hi) for m in range(2) for lo, hi in quarters)

    def scores(j, buf, piece, nk=tk):
        s_ref, mx_ref = buf
        m, lo, hi = piece
        kt = k_ref[pl.ds(pl.multiple_of(j * tk, tk), nk), :]
        s = jnp.dot(kt, qm_ref[m, :, lo:hi], preferred_element_type=F32)
        s_ref[m, 0:nk, lo:hi] = s
        mx_ref[m, :, lo:hi] = jnp.max(s, axis=0, keepdims=True)

    def diag_keys(piece, first_key):
        return min(tk, piece[2] - first_key)

    def consume(j, buf, piece, first_key=None):
        s_ref, mx_ref = buf
        m, lo, hi = piece
        nk = tk if first_key is None else diag_keys(piece, first_key)
        vt = vt_ref[:, pl.ds(pl.multiple_of(j * tk, tk), nk)]
        s = s_ref[m, 0:nk, lo:hi]
        mx = mx_ref[m, :, lo:hi]
        m_old = m_ref[m, :, lo:hi]
        if first_key is not None:
            krow = lax.broadcasted_iota(jnp.int32, s.shape, 0)
            qcol = lax.broadcasted_iota(jnp.int32, s.shape, 1)
            s = jnp.where(krow <= qcol + (lo - first_key), s, -jnp.inf)
            mx = jnp.max(s, axis=0, keepdims=True)
        m_new = jnp.maximum(m_old, mx)
        p = jnp.exp2(s - m_new).astype(BF16)
        acc_ref[m, :, lo:hi] = (jnp.exp2(m_old - m_new) * acc_ref[m, :, lo:hi]
                                + jnp.dot(vt, p, preferred_element_type=F32))
        m_ref[m, :, lo:hi] = m_new

    def start_block(qi):
        qt = qt_ref[:, pl.ds(pl.multiple_of(qi * tq, tq), tq)]
        feat = lax.broadcasted_iota(jnp.int32, qt.shape, 0)
        zero = jnp.zeros_like(qt)
        qm_ref[0] = jnp.where(feat < DIFF_HEAD_DIM, qt, zero)
        qm_ref[1] = jnp.where(feat >= DIFF_HEAD_DIM, qt, zero)
        for pc in pieces:
            scores(0, buf_a, pc)

    start_block(0)

    @pl.loop(0, nq)
    def _(qi):
        acc_ref[...] = jnp.zeros_like(acc_ref)
        m_ref[...] = jnp.full(m_ref.shape, -jnp.inf, F32)

        def tile_pair(j):
            for pc in pieces:
                scores(j + 1, buf_b, pc)
                consume(j, buf_a, pc)
            for pc in pieces:
                scores(j + 2, buf_a, pc)
                consume(j + 1, buf_b, pc)

        @pl.loop(0, qi // 2)
        def _(jj):
            tile_pair(4 * jj)
            tile_pair(4 * jj + 2)

        @pl.when(qi % 2 == 1)
        def _():
            tile_pair(2 * qi - 2)

        for m in range(2):
            q0, q1, q2, q3 = ((m,) + q for q in quarters)
            scores(2 * qi + 1, buf_b, q2, nk=diag_keys(q2, tk))
            consume(2 * qi, buf_a, q0, first_key=0)
            scores(2 * qi + 1, buf_b, q3, nk=diag_keys(q3, tk))
            consume(2 * qi, buf_a, q1, first_key=0)
            consume(2 * qi, buf_a, q2)
            consume(2 * qi, buf_a, q3)
        for m in range(2):
            consume(2 * qi + 1, buf_b, (m,) + quarters[2], first_key=tk)
            consume(2 * qi + 1, buf_b, (m,) + quarters[3], first_key=tk)

        finalize(qi)
        start_block(jnp.minimum(qi + 1, nq - 1))


def _attention(qt, k, vt, lam_qk, nw_col, lambda_init, batch, seq):
    t = k.shape[0]
    tq, tk = ATTN_Q_TILE, ATTN_K_TILE
    nq = seq // tq
    return pl.pallas_call(
        functools.partial(_attn_kernel, lambda_init=lambda_init, nq=nq),
        grid=(batch, DIFF_HEADS),
        in_specs=[
            pl.BlockSpec((LANES, seq), lambda b, h: (h, b)),
            pl.BlockSpec((seq, LANES), lambda b, h: (b, h)),
            pl.BlockSpec((VT_ROWS, seq), lambda b, h: (h, b)),
            pl.BlockSpec((4, DIFF_HEAD_DIM), lambda b, h: (0, 0)),
            pl.BlockSpec((DIFF_V_DIM, 1), lambda b, h: (0, 0)),
        ],
        out_specs=pl.BlockSpec((seq, LANES), lambda b, h: (b, h)),
        out_shape=jax.ShapeDtypeStruct((t, ATTN_V), BF16),
        scratch_shapes=[pltpu.VMEM((2, tk, tq), F32), pltpu.VMEM((2, tk, tq), F32),
                        pltpu.VMEM((2, 1, tq), F32), pltpu.VMEM((2, 1, tq), F32),
                        pltpu.VMEM((2, VT_ROWS, tq), F32), pltpu.VMEM((2, 1, tq), F32),
                        pltpu.VMEM((2, LANES, tq), BF16)],
        compiler_params=_params("parallel", "parallel"),
    )(qt, k, vt, lam_qk, nw_col)


def _layer_norm(r, g, b):
    mu = jnp.mean(r, axis=-1, keepdims=True)
    d = r - mu
    var = jnp.mean(d * d, axis=-1, keepdims=True)
    return d * lax.rsqrt(var + EPS) * g + b


def _mix_ffn_kernel(ys_ref, ya_ref, x_ref, g1_ref, sc2_ref, sh2_ref, g2_ref, wo_ref, wgu_ref, wd_ref,
                    n1g_ref, n1b_ref, n2g_ref, n2b_ref, o_ref, *, alpha):
    tm = x_ref.shape[0]
    halves = [pl.ds(s * (tm // FFN_ROW_SPLIT), tm // FFN_ROW_SPLIT) for s in range(FFN_ROW_SPLIT)]
    dot = functools.partial(jnp.dot, preferred_element_type=F32)

    def outproj(rows):
        return (dot(ys_ref[rows, :], wo_ref[0:SSD_WIDTH, :])
                + dot(ya_ref[rows, :], wo_ref[SSD_WIDTH:D_MODEL, :]))

    def norm_modulate(y, rows):
        x1 = _layer_norm(alpha * x_ref[rows, :] + (1.0 + g1_ref[0]) * y, n1g_ref[...], n1b_ref[...])
        return x1, (x1 * (1.0 + sc2_ref[0]) + sh2_ref[0]).astype(BF16)

    def swiglu(h):
        acc = None
        for c in range(D_FF // FF_TILE):
            lo = c * FF_TILE
            gt = dot(h, wgu_ref[:, lo:lo + FF_TILE])
            up = dot(h, wgu_ref[:, D_FF + lo:D_FF + lo + FF_TILE])
            part = dot((_silu(gt) * up).astype(BF16), wd_ref[lo:lo + FF_TILE, :])
            acc = part if acc is None else acc + part
        return acc

    ys = [outproj(rows) for rows in halves]
    outs = []
    for y, rows in zip(ys, halves):
        x1, h = norm_modulate(y, rows)
        outs.append((x1, swiglu(h)))
    for (x1, acc), rows in zip(outs, halves):
        o_ref[rows, :] = _layer_norm(alpha * x1 + (1.0 + g2_ref[0]) * acc, n2g_ref[...], n2b_ref[...])


def _mix_ffn(ys, ya, x2, mod3, layer, w_out, w_gu, w_down, n1g, n1b, n2g, n2b, alpha, seq):
    t, d = x2.shape
    tm = FFN_ROW_TILE
    per_seq = seq // tm
    row = lambda i: (i, 0)
    const = lambda i: (0, 0)
    mod_row = lambda k: pl.BlockSpec((1, 1, d), lambda i: ((i // per_seq) * N_MOD + k, 0, 0))
    resident = lambda shape: pl.BlockSpec((None,) + shape, lambda i: (layer, 0, 0),
                                          pipeline_mode=pl.Buffered(1))
    return pl.pallas_call(
        functools.partial(_mix_ffn_kernel, alpha=alpha),
        grid=(t // tm,),
        in_specs=[
            pl.BlockSpec((tm, SSD_WIDTH), row),
            pl.BlockSpec((tm, ATTN_V), row),
            pl.BlockSpec((tm, d), row),
            mod_row(2), mod_row(4), mod_row(3), mod_row(5),
            resident((d, d)),
            resident((d, 2 * D_FF)),
            resident((D_FF, d)),
            pl.BlockSpec((1, d), const), pl.BlockSpec((1, d), const),
            pl.BlockSpec((1, d), const), pl.BlockSpec((1, d), const),
        ],
        out_specs=pl.BlockSpec((tm, d), row),
        out_shape=jax.ShapeDtypeStruct((t, d), F32),
        compiler_params=_params("parallel"),
    )(ys, ya, x2, mod3, mod3, mod3, mod3, w_out, w_gu, w_down, n1g, n1b, n2g, n2b)


def _rope_tables(seq):
    dim = DIFF_HEAD_DIM
    inv = 1.0 / (ROPE_THETA ** (jnp.arange(0, dim, 2, dtype=F32) / dim))
    ang = jnp.arange(seq, dtype=F32)[:, None] * inv[None, :]
    cos, sin = jnp.cos(ang), jnp.sin(ang)
    return (jnp.concatenate([cos, cos, cos, cos], -1), jnp.concatenate([-sin, sin, -sin, sin], -1))


def _pack_w_in(w):
    w_a = w[..., 0:ZX_W].astype(BF16)
    w_b = jnp.pad(w[..., ZX_W:], ((0, 0), (0, 0), (DT_LANE0, 0))).astype(BF16)
    return w_a, w_b


def _dt_lanes(v):
    return jnp.pad(v, (DT_LANE0, 0)).reshape(1, LANES)


def kernel(x, c, w_mod, b_mod, w_in, conv_w, conv_b, dt_bias, a_log, d_skip, ssd_norm_w, lam_qk,
           attn_norm_w, w_out, ln1_g, ln1_b, w_gate_up, w_down, ln2_g, ln2_b):
    batch, seq, d = x.shape
    depth = w_mod.shape[0]
    assert d == D_MODEL and seq % ROW_TILE == 0 and seq % ATTN_Q_TILE == 0 and batch <= SUBLANES
    t = batch * seq
    alpha = (2 * depth) ** 0.25

    cos2, sin2 = _rope_tables(seq)
    c_pad = jnp.pad(c, ((0, SUBLANES - batch), (0, 0)))
    mod = _modulation(c_pad, w_mod, b_mod)
    w_in_a, w_in_b = _pack_w_in(w_in)
    w_out_b, w_gu_b, w_down_b = w_out.astype(BF16), w_gate_up.astype(BF16), w_down.astype(BF16)

    x2 = x.reshape(t, d)
    for l in range(depth):
        lambda_init = 0.8 - 0.6 * math.exp(-0.3 * l)
        mod3 = mod[l, :batch].reshape(batch * N_MOD, 1, d)
        y_ssd, q, k, vt = _inproj_ssd(
            x2, mod3, l, w_in_a, w_in_b, cos2, sin2,
            jnp.pad(conv_w[l], ((0, SUBLANES - CONV_WIDTH), (0, 0))),
            conv_b[l].reshape(1, SSD_XBC), _dt_lanes(dt_bias[l]), _dt_lanes(a_log[l]),
            jnp.repeat(d_skip[l], SSD_HEADDIM).reshape(1, SSD_WIDTH),
            ssd_norm_w[l].reshape(1, SSD_WIDTH), seq)
        y_attn = _attention(q, k, vt, lam_qk[l], attn_norm_w[l].reshape(DIFF_V_DIM, 1),
                            lambda_init, batch, seq)
        x2 = _mix_ffn(y_ssd, y_attn, x2, mod3, l, w_out_b, w_gu_b, w_down_b,
                      ln1_g[l].reshape(1, d), ln1_b[l].reshape(1, d),
                      ln2_g[l].reshape(1, d), ln2_b[l].reshape(1, d), alpha, seq)
    return x2.reshape(batch, seq, d)
```

```python
import functools
import math

import jax
import jax.numpy as jnp
from jax import lax
from jax.experimental import pallas as pl
from jax.experimental.pallas import tpu as pltpu

F32 = jnp.float32
BF16 = jnp.bfloat16

D_MODEL = 1024
SSD_WIDTH = 512
SSD_HEADDIM = 64
SSD_HEADS = 8
SSD_GROUPS = 2
SSD_HEADS_PER_GROUP = 4
SSD_STATE = 128
SSD_XBC = SSD_WIDTH + 2 * SSD_GROUPS * SSD_STATE
CONV_WIDTH = 4
CHUNK = 128
DIFF_HEAD_DIM = 64
DIFF_HEADS = 4
DIFF_V_DIM = 128
ATTN_QK = 512
ATTN_V = 512
ROPE_THETA = 10000.0
D_FF = 2816
N_MOD = 6
EPS = 1e-5

LANES = 128
SUBLANES = 8
VMEM_LIMIT_BYTES = 56 * 1024 * 1024

ZX_W = SSD_WIDTH + SSD_XBC
DT_LANE0 = LANES - SSD_HEADS
Q_OFF_B = LANES
K_OFF_B = Q_OFF_B + ATTN_QK
V_OFF_B = K_OFF_B + ATTN_QK
WB_W = V_OFF_B + ATTN_V

BF16_SUBLANES = 16
VT_ROWS = DIFF_V_DIM + BF16_SUBLANES
LOG2E = math.log2(math.e)

ROW_TILE = 512
ATTN_K_TILE = 512
ATTN_Q_TILE = 2 * ATTN_K_TILE
FF_TILE = 2816
FFN_ROW_TILE = 512
FFN_ROW_SPLIT = 2


def _sigmoid(x):
    return 1.0 / (1.0 + jnp.exp(-x))


def _silu(x):
    hx = 0.5 * x
    return hx + hx * jnp.tanh(hx)


def _params(*sem):
    return pltpu.CompilerParams(dimension_semantics=sem, vmem_limit_bytes=VMEM_LIMIT_BYTES)


def _mod_kernel(c_ref, w_ref, b_ref, o_ref):
    c = c_ref[...]
    cond = (c * _sigmoid(c)).astype(BF16)
    o_ref[0] = jnp.dot(cond, w_ref[0].astype(BF16), preferred_element_type=F32) + b_ref[0]


def _modulation(c_pad, w_mod, b_mod):
    depth, d, n = w_mod.shape
    tn = 2048
    return pl.pallas_call(
        _mod_kernel,
        grid=(depth, n // tn),
        in_specs=[
            pl.BlockSpec((SUBLANES, d), lambda l, j: (0, 0)),
            pl.BlockSpec((1, d, tn), lambda l, j: (l, 0, j)),
            pl.BlockSpec((1, 1, tn), lambda l, j: (l, 0, j)),
        ],
        out_specs=pl.BlockSpec((1, SUBLANES, tn), lambda l, j: (l, 0, j)),
        out_shape=jax.ShapeDtypeStruct((depth, SUBLANES, n), F32),
        compiler_params=_params("parallel", "parallel"),
    )(c_pad, w_mod, b_mod.reshape(depth, 1, n))


def _split(x, pieces):
    out = []
    for _ in range(pieces):
        p = x.astype(BF16)
        out.append(p)
        x = x - p.astype(F32)
    return out


def _ssd_chunk(u, z, dt_raw, cw_ref, cb_ref, dtb_ref, alog_ref, dsk_ref, nw_ref, ex_ref, xpad, state):
    L = CHUNK
    dot = functools.partial(jnp.dot, preferred_element_type=F32)

    def taps(v):
        acc = cb_ref[...] + cw_ref[CONV_WIDTH - 1:CONV_WIDTH, :] * v
        for kk in range(CONV_WIDTH - 1):
            acc = acc + cw_ref[kk:kk + 1, :] * pltpu.roll(v, CONV_WIDTH - 1 - kk, 0)
        return acc

    head = taps(jnp.concatenate([xpad[...], u[0:SUBLANES]], axis=0))[SUBLANES:]
    conv = jnp.concatenate([head, taps(u)[SUBLANES:]], axis=0)
    xpad[...] = u[L - SUBLANES:L]
    xc = _silu(conv)
    xs = xc[:, 0:SSD_WIDTH]

    pre = dt_raw + dtb_ref[...]
    dt = jnp.maximum(pre, 0.0) + jnp.log(1.0 + jnp.exp(-jnp.abs(pre)))
    a = dt * (-jnp.exp(alog_ref[...]))
    row = lax.broadcasted_iota(jnp.int32, (L, L), 0)
    col = lax.broadcasted_iota(jnp.int32, (L, L), 1)
    causal = row >= col
    tril = causal.astype(BF16)
    cs = sum(dot(tril, p) for p in _split(a, 3))
    cs_t = cs.T
    expand = lambda v: sum(dot(p, ex_ref[...]) for p in _split(v, 2))
    xdt = xs * expand(dt)
    ecs_x = jnp.exp(expand(cs))
    even_head = (lax.broadcasted_iota(jnp.int32, xdt.shape, 1) // SSD_HEADDIM) % 2 == 0
    xdt_even = jnp.where(even_head, xdt, 0.0).astype(BF16)
    xdt_odd = jnp.where(even_head, 0.0, xdt).astype(BF16)

    ys = []
    for g in range(SSD_GROUPS):
        bm = xc[:, SSD_WIDTH + g * SSD_STATE:SSD_WIDTH + (g + 1) * SSD_STATE]
        cm = xc[:, SSD_WIDTH + (SSD_GROUPS + g) * SSD_STATE:SSD_WIDTH + (SSD_GROUPS + g + 1) * SSD_STATE]
        cm_b = cm.astype(BF16)
        cb = lax.dot_general(cm_b, bm.astype(BF16), (((1,), (1,)), ((), ())),
                             preferred_element_type=F32)
        bm_t = bm.T
        for pp in range(SSD_HEADS_PER_GROUP // 2):
            pr = g * (SSD_HEADS_PER_GROUP // 2) + pp
            halves = (xdt_even[:, pr * LANES:(pr + 1) * LANES],
                      xdt_odd[:, pr * LANES:(pr + 1) * LANES])
            y_pair = None
            st_pair = None
            for half in range(2):
                hd = DT_LANE0 + 2 * pr + half
                cs_col = cs[:, hd:hd + 1]
                cs_row = cs_t[hd:hd + 1, :]
                cs_last = cs_t[hd:hd + 1, L - 1:L]
                lmat = jnp.exp(jnp.where(causal, cs_col - cs_row, -jnp.inf))
                y_h = jnp.dot((cb * lmat).astype(BF16), halves[half], preferred_element_type=F32)
                decay = jnp.exp(cs_last - cs_row)
                st_h = jnp.dot((bm_t * decay).astype(BF16), halves[half],
                               preferred_element_type=F32)
                y_pair = y_h if y_pair is None else y_pair + y_h
                st_pair = st_h if st_pair is None else st_pair + st_h
            prev = state[pr]
            ecs_p = ecs_x[:, pr * LANES:(pr + 1) * LANES]
            y_off = jnp.dot(cm_b, prev.astype(BF16), preferred_element_type=F32) * ecs_p
            state[pr] = ecs_p[L - 1:L, :] * prev + st_pair
            ys.append(y_pair + y_off)

    y = (jnp.concatenate(ys, axis=1) + dsk_ref[...] * xs) * _silu(z)
    gw = SSD_WIDTH // SSD_GROUPS
    out = []
    for g in range(SSD_GROUPS):
        yg = y[:, g * gw:(g + 1) * gw]
        ms = jnp.mean(yg * yg, axis=-1, keepdims=True)
        out.append((yg * lax.rsqrt(ms + EPS) * nw_ref[:, g * gw:(g + 1) * gw]).astype(BF16))
    return jnp.concatenate(out, axis=1)


def _inproj_ssd_kernel(x_ref, sc_ref, sh_ref, wa_ref, wb_ref, cos_ref, sin_ref,
                       cw_ref, cb_ref, dtb_ref, alog_ref, dsk_ref, nw_ref, ex_ref,
                       y_ref, qt_ref, k_ref, vt_ref, xpad, state, *, per_seq):
    @pl.when(pl.program_id(0) % per_seq == 0)
    def _():
        xpad[...] = jnp.zeros_like(xpad)
        state[...] = jnp.zeros_like(state)

    dot = functools.partial(jnp.dot, preferred_element_type=F32)
    h = (x_ref[...] * (1.0 + sc_ref[0]) + sh_ref[0]).astype(BF16)
    cos = cos_ref[...]
    sin = sin_ref[...]
    lane = lax.broadcasted_iota(jnp.int32, cos.shape, 1)
    first_half = (lane % DIFF_HEAD_DIM) < (DIFF_HEAD_DIM // 2)

    def rope(t):
        rot = jnp.where(first_half, pltpu.roll(t, LANES - DIFF_HEAD_DIM // 2, 1),
                        pltpu.roll(t, DIFF_HEAD_DIM // 2, 1))
        return t * cos + rot * sin

    ssd = functools.partial(_ssd_chunk, cw_ref=cw_ref, cb_ref=cb_ref, dtb_ref=dtb_ref,
                            alog_ref=alog_ref, dsk_ref=dsk_ref, nw_ref=nw_ref, ex_ref=ex_ref,
                            xpad=xpad, state=state)

    def ssd_rows(c):
        rows = slice(c * CHUNK, (c + 1) * CHUNK)
        y_ref[rows, :] = ssd(xbc[rows], z[rows], dtq[rows, 0:LANES])

    xbc = dot(h, wa_ref[:, SSD_WIDTH:ZX_W])
    dtq = dot(h, wb_ref[:, 0:K_OFF_B])
    z = dot(h, wa_ref[:, 0:SSD_WIDTH])
    ssd_rows(0)
    k = dot(h, wb_ref[:, K_OFF_B:V_OFF_B])
    ssd_rows(1)
    v = dot(h, wb_ref[:, V_OFF_B:WB_W])
    ssd_rows(2)
    ones = jnp.ones((BF16_SUBLANES, h.shape[0]), BF16)
    for hd in range(DIFF_HEADS):
        lo = hd * LANES
        qh = rope(dtq[:, Q_OFF_B + lo:Q_OFF_B + lo + LANES]) * (LOG2E * DIFF_HEAD_DIM ** -0.5)
        k_ref[:, lo:lo + LANES] = rope(k[:, lo:lo + LANES]).astype(BF16)
        qt_ref[lo:lo + LANES, :] = qh.T.astype(BF16)
        vt_ref[hd * VT_ROWS:hd * VT_ROWS + DIFF_V_DIM, :] = v[:, lo:lo + LANES].T.astype(BF16)
        vt_ref[hd * VT_ROWS + DIFF_V_DIM:(hd + 1) * VT_ROWS, :] = ones
    ssd_rows(3)


def _inproj_ssd(x2, mod3, layer, w_a, w_b, cos2, sin2, conv_w8, conv_b, dtb, alog, dsk, nw, seq):
    t, d = x2.shape
    tm = ROW_TILE
    assert tm == 4 * CHUNK
    per_seq = seq // tm
    row = lambda i: (i, 0)
    const = lambda i: (0, 0)
    head_of_col = jnp.arange(SSD_WIDTH)[None, :] // SSD_HEADDIM
    head = jnp.arange(LANES)[:, None] - DT_LANE0
    expand = (head_of_col == head).astype(BF16)
    return pl.pallas_call(
        functools.partial(_inproj_ssd_kernel, per_seq=per_seq),
        grid=(t // tm,),
        in_specs=[
            pl.BlockSpec((tm, d), row),
            pl.BlockSpec((1, 1, d), lambda i: ((i // per_seq) * N_MOD + 1, 0, 0)),
            pl.BlockSpec((1, 1, d), lambda i: ((i // per_seq) * N_MOD + 0, 0, 0)),
            pl.BlockSpec((None, d, ZX_W), lambda i: (layer, 0, 0)),
            pl.BlockSpec((None, d, WB_W), lambda i: (layer, 0, 0)),
            pl.BlockSpec((tm, LANES), lambda i: (i % per_seq, 0)),
            pl.BlockSpec((tm, LANES), lambda i: (i % per_seq, 0)),
            pl.BlockSpec((SUBLANES, SSD_XBC), const),
            pl.BlockSpec((1, SSD_XBC), const),
            pl.BlockSpec((1, LANES), const),
            pl.BlockSpec((1, LANES), const),
            pl.BlockSpec((1, SSD_WIDTH), const),
            pl.BlockSpec((1, SSD_WIDTH), const),
            pl.BlockSpec((LANES, SSD_WIDTH), const),
        ],
        out_specs=[
            pl.BlockSpec((tm, SSD_WIDTH), row),
            pl.BlockSpec((ATTN_QK, tm), lambda i: (0, i)),
            pl.BlockSpec((tm, ATTN_QK), row),
            pl.BlockSpec((DIFF_HEADS * VT_ROWS, tm), lambda i: (0, i)),
        ],
        out_shape=[
            jax.ShapeDtypeStruct((t, SSD_WIDTH), BF16),
            jax.ShapeDtypeStruct((ATTN_QK, t), BF16),
            jax.ShapeDtypeStruct((t, ATTN_QK), BF16),
            jax.ShapeDtypeStruct((DIFF_HEADS * VT_ROWS, t), BF16),
        ],
        scratch_shapes=[
            pltpu.VMEM((SUBLANES, SSD_XBC), F32),
            pltpu.VMEM((SSD_HEADS // 2, SSD_STATE, 2 * SSD_HEADDIM), F32),
        ],
        compiler_params=_params("arbitrary"),
    )(x2, mod3, mod3, w_a, w_b, cos2, sin2, conv_w8, conv_b, dtb, alog, dsk, nw, expand)


def _attn_kernel(qt_ref, k_ref, vt_ref, lam_ref, nw_ref, o_ref, sa_ref, sb_ref, mxa_ref, mxb_ref,
                 acc_ref, m_ref, qm_ref, *, lambda_init, nq):
    tq, tk = ATTN_Q_TILE, ATTN_K_TILE
    dv = DIFF_V_DIM
    buf_a = (sa_ref, mxa_ref)
    buf_b = (sb_ref, mxb_ref)
    lq = lam_ref[...]
    lam = (jnp.exp(jnp.sum(lq[0:1] * lq[1:2], axis=-1, keepdims=True))
           - jnp.exp(jnp.sum(lq[2:3] * lq[3:4], axis=-1, keepdims=True)) + lambda_init)

    def finalize(blk):
        r0 = 1.0 / acc_ref[0, dv:dv + 1, :]
        r1 = lam / acc_ref[1, dv:dv + 1, :]
        o = acc_ref[0, 0:dv, :] * r0 - acc_ref[1, 0:dv, :] * r1
        ms = jnp.mean(o * o, axis=0, keepdims=True)
        o = o * lax.rsqrt(ms + EPS) * nw_ref[...] * (1.0 - lambda_init)
        o_ref[pl.ds(pl.multiple_of(blk * tq, tq), tq), :] = o.T.astype(BF16)

    quarters = tuple((c * (tk // 2), (c + 1) * (tk // 2)) for c in range(4))
    pieces = tuple((m, lo, hi) for m in range(2) for lo, hi in quarters)

    def scores(j, buf, piece, nk=tk):
        s_ref, mx_ref = buf
        m, lo, hi = piece
        kt = k_ref[pl.ds(pl.multiple_of(j * tk, tk), nk), :]
        s = jnp.dot(kt, qm_ref[m, :, lo:hi], preferred_element_type=F32)
        s_ref[m, 0:nk, lo:hi] = s
        mx_ref[m, :, lo:hi] = jnp.max(s, axis=0, keepdims=True)

    def diag_keys(piece, first_key):
        return min(tk, piece[2] - first_key)

    def consume(j, buf, piece, first_key=None):
        s_ref, mx_ref = buf
        m, lo, hi = piece
        nk = tk if first_key is None else diag_keys(piece, first_key)
        vt = vt_ref[:, pl.ds(pl.multiple_of(j * tk, tk), nk)]
        s = s_ref[m, 0:nk, lo:hi]
        mx = mx_ref[m, :, lo:hi]
        m_old = m_ref[m, :, lo:hi]
        if first_key is not None:
            krow = lax.broadcasted_iota(jnp.int32, s.shape, 0)
            qcol = lax.broadcasted_iota(jnp.int32, s.shape, 1)
            s = jnp.where(krow <= qcol + (lo - first_key), s, -jnp.inf)
            mx = jnp.max(s, axis=0, keepdims=True)
        m_new = jnp.maximum(m_old, mx)
        p = jnp.exp2(s - m_new).astype(BF16)
        acc_ref[m, :, lo:hi] = (jnp.exp2(m_old - m_new) * acc_ref[m, :, lo:hi]
                                + jnp.dot(vt, p, preferred_element_type=F32))
        m_ref[m, :, lo:hi] = m_new

    def start_block(qi):
        qt = qt_ref[:, pl.ds(pl.multiple_of(qi * tq, tq), tq)]
        feat = lax.broadcasted_iota(jnp.int32, qt.shape, 0)
        zero = jnp.zeros_like(qt)
        qm_ref[0] = jnp.where(feat < DIFF_HEAD_DIM, qt, zero)
        qm_ref[1] = jnp.where(feat >= DIFF_HEAD_DIM, qt, zero)
        for pc in pieces:
            scores(0, buf_a, pc)

    start_block(0)

    @pl.loop(0, nq)
    def _(qi):
        acc_ref[...] = jnp.zeros_like(acc_ref)
        m_ref[...] = jnp.full(m_ref.shape, -jnp.inf, F32)

        def tile_pair(j):
            for pc in pieces:
                scores(j + 1, buf_b, pc)
                consume(j, buf_a, pc)
            for pc in pieces:
                scores(j + 2, buf_a, pc)
                consume(j + 1, buf_b, pc)

        @pl.loop(0, qi // 2)
        def _(jj):
            tile_pair(4 * jj)
            tile_pair(4 * jj + 2)

        @pl.when(qi % 2 == 1)
        def _():
            tile_pair(2 * qi - 2)

        for m in range(2):
            q0, q1, q2, q3 = ((m,) + q for q in quarters)
            scores(2 * qi + 1, buf_b, q2, nk=diag_keys(q2, tk))
            consume(2 * qi, buf_a, q0, first_key=0)
            scores(2 * qi + 1, buf_b, q3, nk=diag_keys(q3, tk))
            consume(2 * qi, buf_a, q1, first_key=0)
            consume(2 * qi, buf_a, q2)
            consume(2 * qi, buf_a, q3)
        for m in range(2):
            consume(2 * qi + 1, buf_b, (m,) + quarters[2], first_key=tk)
            consume(2 * qi + 1, buf_b, (m,) + quarters[3], first_key=tk)

        finalize(qi)
        start_block(jnp.minimum(qi + 1, nq - 1))


def _attention(qt, k, vt, lam_qk, nw_col, lambda_init, batch, seq):
    t = k.shape[0]
    tq, tk = ATTN_Q_TILE, ATTN_K_TILE
    nq = seq // tq
    return pl.pallas_call(
        functools.partial(_attn_kernel, lambda_init=lambda_init, nq=nq),
        grid=(batch, DIFF_HEADS),
        in_specs=[
            pl.BlockSpec((LANES, seq), lambda b, h: (h, b)),
            pl.BlockSpec((seq, LANES), lambda b, h: (b, h)),
            pl.BlockSpec((VT_ROWS, seq), lambda b, h: (h, b)),
            pl.BlockSpec((4, DIFF_HEAD_DIM), lambda b, h: (0, 0)),
            pl.BlockSpec((DIFF_V_DIM, 1), lambda b, h: (0, 0)),
        ],
        out_specs=pl.BlockSpec((seq, LANES), lambda b, h: (b, h)),
        out_shape=jax.ShapeDtypeStruct((t, ATTN_V), BF16),
        scratch_shapes=[pltpu.VMEM((2, tk, tq), F32), pltpu.VMEM((2, tk, tq), F32),
                        pltpu.VMEM((2, 1, tq), F32), pltpu.VMEM((2, 1, tq), F32),
                        pltpu.VMEM((2, VT_ROWS, tq), F32), pltpu.VMEM((2, 1, tq), F32),
                        pltpu.VMEM((2, LANES, tq), BF16)],
        compiler_params=_params("parallel", "parallel"),
    )(qt, k, vt, lam_qk, nw_col)


def _layer_norm(r, g, b):
    mu = jnp.mean(r, axis=-1, keepdims=True)
    d = r - mu
    var = jnp.mean(d * d, axis=-1, keepdims=True)
    return d * lax.rsqrt(var + EPS) * g + b


def _mix_ffn_kernel(ys_ref, ya_ref, x_ref, g1_ref, sc2_ref, sh2_ref, g2_ref, wo_ref, wgu_ref, wd_ref,
                    n1g_ref, n1b_ref, n2g_ref, n2b_ref, o_ref, *, alpha):
    tm = x_ref.shape[0]
    halves = [pl.ds(s * (tm // FFN_ROW_SPLIT), tm // FFN_ROW_SPLIT) for s in range(FFN_ROW_SPLIT)]
    dot = functools.partial(jnp.dot, preferred_element_type=F32)

    def outproj(rows):
        return (dot(ys_ref[rows, :], wo_ref[0:SSD_WIDTH, :])
                + dot(ya_ref[rows, :], wo_ref[SSD_WIDTH:D_MODEL, :]))

    def norm_modulate(y, rows):
        x1 = _layer_norm(alpha * x_ref[rows, :] + (1.0 + g1_ref[0]) * y, n1g_ref[...], n1b_ref[...])
        return x1, (x1 * (1.0 + sc2_ref[0]) + sh2_ref[0]).astype(BF16)

    def swiglu(h):
        acc = None
        for c in range(D_FF // FF_TILE):
            lo = c * FF_TILE
            gt = dot(h, wgu_ref[:, lo:lo + FF_TILE])
            up = dot(h, wgu_ref[:, D_FF + lo:D_FF + lo + FF_TILE])
            part = dot((_silu(gt) * up).astype(BF16), wd_ref[lo:lo + FF_TILE, :])
            acc = part if acc is None else acc + part
        return acc

    ys = [outproj(rows) for rows in halves]
    outs = []
    for y, rows in zip(ys, halves):
        x1, h = norm_modulate(y, rows)
        outs.append((x1, swiglu(h)))
    for (x1, acc), rows in zip(outs, halves):
        o_ref[rows, :] = _layer_norm(alpha * x1 + (1.0 + g2_ref[0]) * acc, n2g_ref[...], n2b_ref[...])


def _mix_ffn(ys, ya, x2, mod3, layer, w_out, w_gu, w_down, n1g, n1b, n2g, n2b, alpha, seq):
    t, d = x2.shape
    tm = FFN_ROW_TILE
    per_seq = seq // tm
    row = lambda i: (i, 0)
    const = lambda i: (0, 0)
    mod_row = lambda k: pl.BlockSpec((1, 1, d), lambda i: ((i // per_seq) * N_MOD + k, 0, 0))
    resident = lambda shape: pl.BlockSpec((None,) + shape, lambda i: (layer, 0, 0),
                                          pipeline_mode=pl.Buffered(1))
    return pl.pallas_call(
        functools.partial(_mix_ffn_kernel, alpha=alpha),
        grid=(t // tm,),
        in_specs=[
            pl.BlockSpec((tm, SSD_WIDTH), row),
            pl.BlockSpec((tm, ATTN_V), row),
            pl.BlockSpec((tm, d), row),
            mod_row(2), mod_row(4), mod_row(3), mod_row(5),
            resident((d, d)),
            resident((d, 2 * D_FF)),
            resident((D_FF, d)),
            pl.BlockSpec((1, d), const), pl.BlockSpec((1, d), const),
            pl.BlockSpec((1, d), const), pl.BlockSpec((1, d), const),
        ],
        out_specs=pl.BlockSpec((tm, d), row),
        out_shape=jax.ShapeDtypeStruct((t, d), F32),
        compiler_params=_params("parallel"),
    )(ys, ya, x2, mod3, mod3, mod3, mod3, w_out, w_gu, w_down, n1g, n1b, n2g, n2b)


def _rope_tables(seq):
    dim = DIFF_HEAD_DIM
    inv = 1.0 / (ROPE_THETA ** (jnp.arange(0, dim, 2, dtype=F32) / dim))
    ang = jnp.arange(seq, dtype=F32)[:, None] * inv[None, :]
    cos, sin = jnp.cos(ang), jnp.sin(ang)
    return (jnp.concatenate([cos, cos, cos, cos], -1), jnp.concatenate([-sin, sin, -sin, sin], -1))


def _pack_w_in(w):
    w_a = w[..., 0:ZX_W].astype(BF16)
    w_b = jnp.pad(w[..., ZX_W:], ((0, 0), (0, 0), (DT_LANE0, 0))).astype(BF16)
    return w_a, w_b


def _dt_lanes(v):
    return jnp.pad(v, (DT_LANE0, 0)).reshape(1, LANES)


def kernel(x, c, w_mod, b_mod, w_in, conv_w, conv_b, dt_bias, a_log, d_skip, ssd_norm_w, lam_qk,
           attn_norm_w, w_out, ln1_g, ln1_b, w_gate_up, w_down, ln2_g, ln2_b):
    batch, seq, d = x.shape
    depth = w_mod.shape[0]
    assert d == D_MODEL and seq % ROW_TILE == 0 and seq % ATTN_Q_TILE == 0 and batch <= SUBLANES
    t = batch * seq
    alpha = (2 * depth) ** 0.25

    cos2, sin2 = _rope_tables(seq)
    c_pad = jnp.pad(c, ((0, SUBLANES - batch), (0, 0)))
    mod = _modulation(c_pad, w_mod, b_mod)
    w_in_a, w_in_b = _pack_w_in(w_in)
    w_out_b, w_gu_b, w_down_b = w_out.astype(BF16), w_gate_up.astype(BF16), w_down.astype(BF16)

    x2 = x.reshape(t, d)
    for l in range(depth):
        lambda_init = 0.8 - 0.6 * math.exp(-0.3 * l)
        mod3 = mod[l, :batch].reshape(batch * N_MOD, 1, d)
        y_ssd, q, k, vt = _inproj_ssd(
            x2, mod3, l, w_in_a, w_in_b, cos2, sin2,
            jnp.pad(conv_w[l], ((0, SUBLANES - CONV_WIDTH), (0, 0))),
            conv_b[l].reshape(1, SSD_XBC), _dt_lanes(dt_bias[l]), _dt_lanes(a_log[l]),
            jnp.repeat(d_skip[l], SSD_HEADDIM).reshape(1, SSD_WIDTH),
            ssd_norm_w[l].reshape(1, SSD_WIDTH), seq)
        y_attn = _attention(q, k, vt, lam_qk[l], attn_norm_w[l].reshape(DIFF_V_DIM, 1),
                            lambda_init, batch, seq)
        x2 = _mix_ffn(y_ssd, y_attn, x2, mod3, l, w_out_b, w_gu_b, w_down_b,
                      ln1_g[l].reshape(1, d), ln1_b[l].reshape(1, d),
                      ln2_g[l].reshape(1, d), ln2_b[l].reshape(1, d), alpha, seq)
    return x2.reshape(batch, seq, d)
```

```python
import functools
import math

import jax
import jax.numpy as jnp
from jax import lax
from jax.experimental import pallas as pl
from jax.experimental.pallas import tpu as pltpu

F32 = jnp.float32
BF16 = jnp.bfloat16

D_MODEL = 1024
SSD_WIDTH = 512
SSD_HEADDIM = 64
SSD_HEADS = 8
SSD_GROUPS = 2
SSD_HEADS_PER_GROUP = 4
SSD_STATE = 128
SSD_XBC = SSD_WIDTH + 2 * SSD_GROUPS * SSD_STATE
CONV_WIDTH = 4
CHUNK = 128
DIFF_HEAD_DIM = 64
DIFF_HEADS = 4
DIFF_V_DIM = 128
ATTN_QK = 512
ATTN_V = 512
ROPE_THETA = 10000.0
D_FF = 2816
N_MOD = 6
EPS = 1e-5

LANES = 128
SUBLANES = 8
VMEM_LIMIT_BYTES = 56 * 1024 * 1024

ZX_W = SSD_WIDTH + SSD_XBC
DT_LANE0 = LANES - SSD_HEADS
Q_OFF_B = LANES
K_OFF_B = Q_OFF_B + ATTN_QK
V_OFF_B = K_OFF_B + ATTN_QK
WB_W = V_OFF_B + ATTN_V

BF16_SUBLANES = 16
VT_ROWS = DIFF_V_DIM + BF16_SUBLANES
LOG2E = math.log2(math.e)

ROW_TILE = 512
ATTN_K_TILE = 512
ATTN_Q_TILE = 2 * ATTN_K_TILE
FF_TILE = 2816
FFN_ROW_TILE = 1024
FFN_ROW_SPLIT = 4


def _sigmoid(x):
    return 1.0 / (1.0 + jnp.exp(-x))


def _silu(x):
    hx = 0.5 * x
    return hx + hx * jnp.tanh(hx)


def _params(*sem):
    return pltpu.CompilerParams(dimension_semantics=sem, vmem_limit_bytes=VMEM_LIMIT_BYTES)


def _mod_kernel(c_ref, w_ref, b_ref, o_ref):
    c = c_ref[...]
    cond = (c * _sigmoid(c)).astype(BF16)
    o_ref[0] = jnp.dot(cond, w_ref[0].astype(BF16), preferred_element_type=F32) + b_ref[0]


def _modulation(c_pad, w_mod, b_mod):
    depth, d, n = w_mod.shape
    tn = 2048
    return pl.pallas_call(
        _mod_kernel,
        grid=(depth, n // tn),
        in_specs=[
            pl.BlockSpec((SUBLANES, d), lambda l, j: (0, 0)),
            pl.BlockSpec((1, d, tn), lambda l, j: (l, 0, j)),
            pl.BlockSpec((1, 1, tn), lambda l, j: (l, 0, j)),
        ],
        out_specs=pl.BlockSpec((1, SUBLANES, tn), lambda l, j: (l, 0, j)),
        out_shape=jax.ShapeDtypeStruct((depth, SUBLANES, n), F32),
        compiler_params=_params("parallel", "parallel"),
    )(c_pad, w_mod, b_mod.reshape(depth, 1, n))


def _split(x, pieces):
    out = []
    for _ in range(pieces):
        p = x.astype(BF16)
        out.append(p)
        x = x - p.astype(F32)
    return out


def _ssd_chunk(u, z, dt_raw, cw_ref, cb_ref, dtb_ref, alog_ref, dsk_ref, nw_ref, ex_ref, xpad, state):
    L = CHUNK
    dot = functools.partial(jnp.dot, preferred_element_type=F32)

    def taps(v):
        acc = cb_ref[...] + cw_ref[CONV_WIDTH - 1:CONV_WIDTH, :] * v
        for kk in range(CONV_WIDTH - 1):
            acc = acc + cw_ref[kk:kk + 1, :] * pltpu.roll(v, CONV_WIDTH - 1 - kk, 0)
        return acc

    head = taps(jnp.concatenate([xpad[...], u[0:SUBLANES]], axis=0))[SUBLANES:]
    conv = jnp.concatenate([head, taps(u)[SUBLANES:]], axis=0)
    xpad[...] = u[L - SUBLANES:L]
    xc = _silu(conv)
    xs = xc[:, 0:SSD_WIDTH]

    pre = dt_raw + dtb_ref[...]
    dt = jnp.maximum(pre, 0.0) + jnp.log(1.0 + jnp.exp(-jnp.abs(pre)))
    a = dt * (-jnp.exp(alog_ref[...]))
    row = lax.broadcasted_iota(jnp.int32, (L, L), 0)
    col = lax.broadcasted_iota(jnp.int32, (L, L), 1)
    causal = row >= col
    tril = causal.astype(BF16)
    cs = sum(dot(tril, p) for p in _split(a, 3))
    cs_t = cs.T
    expand = lambda v: sum(dot(p, ex_ref[...]) for p in _split(v, 2))
    xdt = xs * expand(dt)
    ecs_x = jnp.exp(expand(cs))
    even_head = (lax.broadcasted_iota(jnp.int32, xdt.shape, 1) // SSD_HEADDIM) % 2 == 0
    xdt_even = jnp.where(even_head, xdt, 0.0).astype(BF16)
    xdt_odd = jnp.where(even_head, 0.0, xdt).astype(BF16)

    ys = []
    for g in range(SSD_GROUPS):
        bm = xc[:, SSD_WIDTH + g * SSD_STATE:SSD_WIDTH + (g + 1) * SSD_STATE]
        cm = xc[:, SSD_WIDTH + (SSD_GROUPS + g) * SSD_STATE:SSD_WIDTH + (SSD_GROUPS + g + 1) * SSD_STATE]
        cm_b = cm.astype(BF16)
        cb = lax.dot_general(cm_b, bm.astype(BF16), (((1,), (1,)), ((), ())),
                             preferred_element_type=F32)
        bm_t = bm.T
        for pp in range(SSD_HEADS_PER_GROUP // 2):
            pr = g * (SSD_HEADS_PER_GROUP // 2) + pp
            halves = (xdt_even[:, pr * LANES:(pr + 1) * LANES],
                      xdt_odd[:, pr * LANES:(pr + 1) * LANES])
            y_pair = None
            st_pair = None
            for half in range(2):
                hd = DT_LANE0 + 2 * pr + half
                cs_col = cs[:, hd:hd + 1]
                cs_row = cs_t[hd:hd + 1, :]
                cs_last = cs_t[hd:hd + 1, L - 1:L]
                lmat = jnp.exp(jnp.where(causal, cs_col - cs_row, -jnp.inf))
                y_h = jnp.dot((cb * lmat).astype(BF16), halves[half], preferred_element_type=F32)
                decay = jnp.exp(cs_last - cs_row)
                st_h = jnp.dot((bm_t * decay).astype(BF16), halves[half],
                               preferred_element_type=F32)
                y_pair = y_h if y_pair is None else y_pair + y_h
                st_pair = st_h if st_pair is None else st_pair + st_h
            prev = state[pr]
            ecs_p = ecs_x[:, pr * LANES:(pr + 1) * LANES]
            y_off = jnp.dot(cm_b, prev.astype(BF16), preferred_element_type=F32) * ecs_p
            state[pr] = ecs_p[L - 1:L, :] * prev + st_pair
            ys.append(y_pair + y_off)

    y = (jnp.concatenate(ys, axis=1) + dsk_ref[...] * xs) * _silu(z)
    gw = SSD_WIDTH // SSD_GROUPS
    out = []
    for g in range(SSD_GROUPS):
        yg = y[:, g * gw:(g + 1) * gw]
        ms = jnp.mean(yg * yg, axis=-1, keepdims=True)
        out.append((yg * lax.rsqrt(ms + EPS) * nw_ref[:, g * gw:(g + 1) * gw]).astype(BF16))
    return jnp.concatenate(out, axis=1)


def _inproj_ssd_kernel(x_ref, sc_ref, sh_ref, wa_ref, wb_ref, cos_ref, sin_ref,
                       cw_ref, cb_ref, dtb_ref, alog_ref, dsk_ref, nw_ref, ex_ref,
                       y_ref, qt_ref, k_ref, vt_ref, xpad, state, *, per_seq):
    @pl.when(pl.program_id(0) % per_seq == 0)
    def _():
        xpad[...] = jnp.zeros_like(xpad)
        state[...] = jnp.zeros_like(state)

    dot = functools.partial(jnp.dot, preferred_element_type=F32)
    h = (x_ref[...] * (1.0 + sc_ref[0]) + sh_ref[0]).astype(BF16)
    cos = cos_ref[...]
    sin = sin_ref[...]
    lane = lax.broadcasted_iota(jnp.int32, cos.shape, 1)
    first_half = (lane % DIFF_HEAD_DIM) < (DIFF_HEAD_DIM // 2)

    def rope(t):
        rot = jnp.where(first_half, pltpu.roll(t, LANES - DIFF_HEAD_DIM // 2, 1),
                        pltpu.roll(t, DIFF_HEAD_DIM // 2, 1))
        return t * cos + rot * sin

    ssd = functools.partial(_ssd_chunk, cw_ref=cw_ref, cb_ref=cb_ref, dtb_ref=dtb_ref,
                            alog_ref=alog_ref, dsk_ref=dsk_ref, nw_ref=nw_ref, ex_ref=ex_ref,
                            xpad=xpad, state=state)

    def ssd_rows(c):
        rows = slice(c * CHUNK, (c + 1) * CHUNK)
        y_ref[rows, :] = ssd(xbc[rows], z[rows], dtq[rows, 0:LANES])

    xbc = dot(h, wa_ref[:, SSD_WIDTH:ZX_W])
    dtq = dot(h, wb_ref[:, 0:K_OFF_B])
    z = dot(h, wa_ref[:, 0:SSD_WIDTH])
    ssd_rows(0)
    k = dot(h, wb_ref[:, K_OFF_B:V_OFF_B])
    ssd_rows(1)
    v = dot(h, wb_ref[:, V_OFF_B:WB_W])
    ssd_rows(2)
    ones = jnp.ones((BF16_SUBLANES, h.shape[0]), BF16)
    for hd in range(DIFF_HEADS):
        lo = hd * LANES
        qh = rope(dtq[:, Q_OFF_B + lo:Q_OFF_B + lo + LANES]) * (LOG2E * DIFF_HEAD_DIM ** -0.5)
        k_ref[:, lo:lo + LANES] = rope(k[:, lo:lo + LANES]).astype(BF16)
        qt_ref[lo:lo + LANES, :] = qh.T.astype(BF16)
        vt_ref[hd * VT_ROWS:hd * VT_ROWS + DIFF_V_DIM, :] = v[:, lo:lo + LANES].T.astype(BF16)
        vt_ref[hd * VT_ROWS + DIFF_V_DIM:(hd + 1) * VT_ROWS, :] = ones
    ssd_rows(3)


def _inproj_ssd(x2, mod3, layer, w_a, w_b, cos2, sin2, conv_w8, conv_b, dtb, alog, dsk, nw, seq):
    t, d = x2.shape
    tm = ROW_TILE
    assert tm == 4 * CHUNK
    per_seq = seq // tm
    row = lambda i: (i, 0)
    const = lambda i: (0, 0)
    head_of_col = jnp.arange(SSD_WIDTH)[None, :] // SSD_HEADDIM
    head = jnp.arange(LANES)[:, None] - DT_LANE0
    expand = (head_of_col == head).astype(BF16)
    return pl.pallas_call(
        functools.partial(_inproj_ssd_kernel, per_seq=per_seq),
        grid=(t // tm,),
        in_specs=[
            pl.BlockSpec((tm, d), row),
            pl.BlockSpec((1, 1, d), lambda i: ((i // per_seq) * N_MOD + 1, 0, 0)),
            pl.BlockSpec((1, 1, d), lambda i: ((i // per_seq) * N_MOD + 0, 0, 0)),
            pl.BlockSpec((None, d, ZX_W), lambda i: (layer, 0, 0)),
            pl.BlockSpec((None, d, WB_W), lambda i: (layer, 0, 0)),
            pl.BlockSpec((tm, LANES), lambda i: (i % per_seq, 0)),
            pl.BlockSpec((tm, LANES), lambda i: (i % per_seq, 0)),
            pl.BlockSpec((SUBLANES, SSD_XBC), const),
            pl.BlockSpec((1, SSD_XBC), const),
            pl.BlockSpec((1, LANES), const),
            pl.BlockSpec((1, LANES), const),
            pl.BlockSpec((1, SSD_WIDTH), const),
            pl.BlockSpec((1, SSD_WIDTH), const),
            pl.BlockSpec((LANES, SSD_WIDTH), const),
        ],
        out_specs=[
            pl.BlockSpec((tm, SSD_WIDTH), row),
            pl.BlockSpec((ATTN_QK, tm), lambda i: (0, i)),
            pl.BlockSpec((tm, ATTN_QK), row),
            pl.BlockSpec((DIFF_HEADS * VT_ROWS, tm), lambda i: (0, i)),
        ],
        out_shape=[
            jax.ShapeDtypeStruct((t, SSD_WIDTH), BF16),
            jax.ShapeDtypeStruct((ATTN_QK, t), BF16),
            jax.ShapeDtypeStruct((t, ATTN_QK), BF16),
            jax.ShapeDtypeStruct((DIFF_HEADS * VT_ROWS, t), BF16),
        ],
        scratch_shapes=[
            pltpu.VMEM((SUBLANES, SSD_XBC), F32),
            pltpu.VMEM((SSD_HEADS // 2, SSD_STATE, 2 * SSD_HEADDIM), F32),
        ],
        compiler_params=_params("arbitrary"),
    )(x2, mod3, mod3, w_a, w_b, cos2, sin2, conv_w8, conv_b, dtb, alog, dsk, nw, expand)


def _attn_kernel(qt_ref, k_ref, vt_ref, lam_ref, nw_ref, o_ref, sa_ref, sb_ref, mxa_ref, mxb_ref,
                 acc_ref, m_ref, qm_ref, *, lambda_init, nq):
    tq, tk = ATTN_Q_TILE, ATTN_K_TILE
    dv = DIFF_V_DIM
    buf_a = (sa_ref, mxa_ref)
    buf_b = (sb_ref, mxb_ref)
    lq = lam_ref[...]
    lam = (jnp.exp(jnp.sum(lq[0:1] * lq[1:2], axis=-1, keepdims=True))
           - jnp.exp(jnp.sum(lq[2:3] * lq[3:4], axis=-1, keepdims=True)) + lambda_init)

    def finalize(blk):
        r0 = 1.0 / acc_ref[0, dv:dv + 1, :]
        r1 = lam / acc_ref[1, dv:dv + 1, :]
        o = acc_ref[0, 0:dv, :] * r0 - acc_ref[1, 0:dv, :] * r1
        ms = jnp.mean(o * o, axis=0, keepdims=True)
        o = o * lax.rsqrt(ms + EPS) * nw_ref[...] * (1.0 - lambda_init)
        o_ref[pl.ds(pl.multiple_of(blk * tq, tq), tq), :] = o.T.astype(BF16)

    quarters = tuple((c * (tk // 2), (c + 1) * (tk // 2)) for c in range(4))
    pieces = tuple((m, lo, hi) for m in range(2) for lo, hi in quarters)

    def scores(j, buf, piece, nk=tk):
        s_ref, mx_ref = buf
        m, lo, hi = piece
        kt = k_ref[pl.ds(pl.multiple_of(j * tk, tk), nk), :]
        s = jnp.dot(kt, qm_ref[m, :, lo:hi], preferred_element_type=F32)
        s_ref[m, 0:nk, lo:hi] = s
        mx_ref[m, :, lo:hi] = jnp.max(s, axis=0, keepdims=True)

    def diag_keys(piece, first_key):
        return min(tk, piece[2] - first_key)

    def consume(j, buf, piece, first_key=None):
        s_ref, mx_ref = buf
        m, lo, hi = piece
        nk = tk if first_key is None else diag_keys(piece, first_key)
        vt = vt_ref[:, pl.ds(pl.multiple_of(j * tk, tk), nk)]
        s = s_ref[m, 0:nk, lo:hi]
        mx = mx_ref[m, :, lo:hi]
        m_old = m_ref[m, :, lo:hi]
        if first_key is not None:
            krow = lax.broadcasted_iota(jnp.int32, s.shape, 0)
            qcol = lax.broadcasted_iota(jnp.int32, s.shape, 1)
            s = jnp.where(krow <= qcol + (lo - first_key), s, -jnp.inf)
            mx = jnp.max(s, axis=0, keepdims=True)
        m_new = jnp.maximum(m_old, mx)
        p = jnp.exp2(s - m_new).astype(BF16)
        acc_ref[m, :, lo:hi] = (jnp.exp2(m_old - m_new) * acc_ref[m, :, lo:hi]
                                + jnp.dot(vt, p, preferred_element_type=F32))
        m_ref[m, :, lo:hi] = m_new

    def start_block(qi):
        qt = qt_ref[:, pl.ds(pl.multiple_of(qi * tq, tq), tq)]
        feat = lax.broadcasted_iota(jnp.int32, qt.shape, 0)
        zero = jnp.zeros_like(qt)
        qm_ref[0] = jnp.where(feat < DIFF_HEAD_DIM, qt, zero)
        qm_ref[1] = jnp.where(feat >= DIFF_HEAD_DIM, qt, zero)
        for pc in pieces:
            scores(0, buf_a, pc)

    start_block(0)

    @pl.loop(0, nq)
    def _(qi):
        acc_ref[...] = jnp.zeros_like(acc_ref)
        m_ref[...] = jnp.full(m_ref.shape, -jnp.inf, F32)

        def tile_pair(j):
            for pc in pieces:
                scores(j + 1, buf_b, pc)
                consume(j, buf_a, pc)
            for pc in pieces:
                scores(j + 2, buf_a, pc)
                consume(j + 1, buf_b, pc)

        @pl.loop(0, qi // 2)
        def _(jj):
            tile_pair(4 * jj)
            tile_pair(4 * jj + 2)

        @pl.when(qi % 2 == 1)
        def _():
            tile_pair(2 * qi - 2)

        for m in range(2):
            q0, q1, q2, q3 = ((m,) + q for q in quarters)
            scores(2 * qi + 1, buf_b, q2, nk=diag_keys(q2, tk))
            consume(2 * qi, buf_a, q0, first_key=0)
            scores(2 * qi + 1, buf_b, q3, nk=diag_keys(q3, tk))
            consume(2 * qi, buf_a, q1, first_key=0)
            consume(2 * qi, buf_a, q2)
            consume(2 * qi, buf_a, q3)
        for m in range(2):
            consume(2 * qi + 1, buf_b, (m,) + quarters[2], first_key=tk)
            consume(2 * qi + 1, buf_b, (m,) + quarters[3], first_key=tk)

        finalize(qi)
        start_block(jnp.minimum(qi + 1, nq - 1))


def _attention(qt, k, vt, lam_qk, nw_col, lambda_init, batch, seq):
    t = k.shape[0]
    tq, tk = ATTN_Q_TILE, ATTN_K_TILE
    nq = seq // tq
    return pl.pallas_call(
        functools.partial(_attn_kernel, lambda_init=lambda_init, nq=nq),
        grid=(batch, DIFF_HEADS),
        in_specs=[
            pl.BlockSpec((LANES, seq), lambda b, h: (h, b)),
            pl.BlockSpec((seq, LANES), lambda b, h: (b, h)),
            pl.BlockSpec((VT_ROWS, seq), lambda b, h: (h, b)),
            pl.BlockSpec((4, DIFF_HEAD_DIM), lambda b, h: (0, 0)),
            pl.BlockSpec((DIFF_V_DIM, 1), lambda b, h: (0, 0)),
        ],
        out_specs=pl.BlockSpec((seq, LANES), lambda b, h: (b, h)),
        out_shape=jax.ShapeDtypeStruct((t, ATTN_V), BF16),
        scratch_shapes=[pltpu.VMEM((2, tk, tq), F32), pltpu.VMEM((2, tk, tq), F32),
                        pltpu.VMEM((2, 1, tq), F32), pltpu.VMEM((2, 1, tq), F32),
                        pltpu.VMEM((2, VT_ROWS, tq), F32), pltpu.VMEM((2, 1, tq), F32),
                        pltpu.VMEM((2, LANES, tq), BF16)],
        compiler_params=_params("parallel", "parallel"),
    )(qt, k, vt, lam_qk, nw_col)


def _layer_norm(r, g, b):
    mu = jnp.mean(r, axis=-1, keepdims=True)
    d = r - mu
    var = jnp.mean(d * d, axis=-1, keepdims=True)
    return d * lax.rsqrt(var + EPS) * g + b


def _mix_ffn_kernel(ys_ref, ya_ref, x_ref, g1_ref, sc2_ref, sh2_ref, g2_ref, wo_ref, wgu_ref, wd_ref,
                    n1g_ref, n1b_ref, n2g_ref, n2b_ref, o_ref, *, alpha):
    tm = x_ref.shape[0]
    halves = [pl.ds(s * (tm // FFN_ROW_SPLIT), tm // FFN_ROW_SPLIT) for s in range(FFN_ROW_SPLIT)]
    dot = functools.partial(jnp.dot, preferred_element_type=F32)

    def outproj(rows):
        return (dot(ys_ref[rows, :], wo_ref[0:SSD_WIDTH, :])
                + dot(ya_ref[rows, :], wo_ref[SSD_WIDTH:D_MODEL, :]))

    def norm_modulate(y, rows):
        x1 = _layer_norm(alpha * x_ref[rows, :] + (1.0 + g1_ref[0]) * y, n1g_ref[...], n1b_ref[...])
        return x1, (x1 * (1.0 + sc2_ref[0]) + sh2_ref[0]).astype(BF16)

    def swiglu(h):
        acc = None
        for c in range(D_FF // FF_TILE):
            lo = c * FF_TILE
            gt = dot(h, wgu_ref[:, lo:lo + FF_TILE])
            up = dot(h, wgu_ref[:, D_FF + lo:D_FF + lo + FF_TILE])
            part = dot((_silu(gt) * up).astype(BF16), wd_ref[lo:lo + FF_TILE, :])
            acc = part if acc is None else acc + part
        return acc

    ys = [outproj(rows) for rows in halves]
    outs = []
    for y, rows in zip(ys, halves):
        x1, h = norm_modulate(y, rows)
        outs.append((x1, swiglu(h)))
    for (x1, acc), rows in zip(outs, halves):
        o_ref[rows, :] = _layer_norm(alpha * x1 + (1.0 + g2_ref[0]) * acc, n2g_ref[...], n2b_ref[...])


def _mix_ffn(ys, ya, x2, mod3, layer, w_out, w_gu, w_down, n1g, n1b, n2g, n2b, alpha, seq):
    t, d = x2.shape
    tm = FFN_ROW_TILE
    per_seq = seq // tm
    row = lambda i: (i, 0)
    const = lambda i: (0, 0)
    mod_row = lambda k: pl.BlockSpec((1, 1, d), lambda i: ((i // per_seq) * N_MOD + k, 0, 0))
    resident = lambda shape: pl.BlockSpec((None,) + shape, lambda i: (layer, 0, 0),
                                          pipeline_mode=pl.Buffered(1))
    return pl.pallas_call(
        functools.partial(_mix_ffn_kernel, alpha=alpha),
        grid=(t // tm,),
        in_specs=[
            pl.BlockSpec((tm, SSD_WIDTH), row),
            pl.BlockSpec((tm, ATTN_V), row),
            pl.BlockSpec((tm, d), row),
            mod_row(2), mod_row(4), mod_row(3), mod_row(5),
            resident((d, d)),
            resident((d, 2 * D_FF)),
            resident((D_FF, d)),
            pl.BlockSpec((1, d), const), pl.BlockSpec((1, d), const),
            pl.BlockSpec((1, d), const), pl.BlockSpec((1, d), const),
        ],
        out_specs=pl.BlockSpec((tm, d), row),
        out_shape=jax.ShapeDtypeStruct((t, d), F32),
        compiler_params=_params("parallel"),
    )(ys, ya, x2, mod3, mod3, mod3, mod3, w_out, w_gu, w_down, n1g, n1b, n2g, n2b)


def _rope_tables(seq):
    dim = DIFF_HEAD_DIM
    inv = 1.0 / (ROPE_THETA ** (jnp.arange(0, dim, 2, dtype=F32) / dim))
    ang = jnp.arange(seq, dtype=F32)[:, None] * inv[None, :]
    cos, sin = jnp.cos(ang), jnp.sin(ang)
    return (jnp.concatenate([cos, cos, cos, cos], -1), jnp.concatenate([-sin, sin, -sin, sin], -1))


def _pack_w_in(w):
    w_a = w.astype(BF16)
    w_b = jnp.pad(w_a[..., ZX_W:], ((0, 0), (0, 0), (DT_LANE0, 0)))
    return w_a, w_b


def _dt_lanes(v):
    return jnp.pad(v, (DT_LANE0, 0)).reshape(1, LANES)


def kernel(x, c, w_mod, b_mod, w_in, conv_w, conv_b, dt_bias, a_log, d_skip, ssd_norm_w, lam_qk,
           attn_norm_w, w_out, ln1_g, ln1_b, w_gate_up, w_down, ln2_g, ln2_b):
    batch, seq, d = x.shape
    depth = w_mod.shape[0]
    assert d == D_MODEL and seq % ROW_TILE == 0 and seq % ATTN_Q_TILE == 0 and batch <= SUBLANES
    t = batch * seq
    alpha = (2 * depth) ** 0.25

    cos2, sin2 = _rope_tables(seq)
    c_pad = jnp.pad(c, ((0, SUBLANES - batch), (0, 0)))
    mod = _modulation(c_pad, w_mod, b_mod)
    w_in_a, w_in_b = _pack_w_in(w_in)
    w_out_b, w_gu_b, w_down_b = w_out.astype(BF16), w_gate_up.astype(BF16), w_down.astype(BF16)

    x2 = x.reshape(t, d)
    for l in range(depth):
        lambda_init = 0.8 - 0.6 * math.exp(-0.3 * l)
        mod3 = mod[l, :batch].reshape(batch * N_MOD, 1, d)
        y_ssd, q, k, vt = _inproj_ssd(
            x2, mod3, l, w_in_a, w_in_b, cos2, sin2,
            jnp.pad(conv_w[l], ((0, SUBLANES - CONV_WIDTH), (0, 0))),
            conv_b[l].reshape(1, SSD_XBC), _dt_lanes(dt_bias[l]), _dt_lanes(a_log[l]),
            jnp.repeat(d_skip[l], SSD_HEADDIM).reshape(1, SSD_WIDTH),
            ssd_norm_w[l].reshape(1, SSD_WIDTH), seq)
        y_attn = _attention(q, k, vt, lam_qk[l], attn_norm_w[l].reshape(DIFF_V_DIM, 1),
                            lambda_init, batch, seq)
        x2 = _mix_ffn(y_ssd, y_attn, x2, mod3, l, w_out_b, w_gu_b, w_down_b,
                      ln1_g[l].reshape(1, d), ln1_b[l].reshape(1, d),
                      ln2_g[l].reshape(1, d), ln2_b[l].reshape(1, d), alpha, seq)
    return x2.reshape(batch, seq, d)
```

```python
import functools
import math

import jax
import jax.numpy as jnp
from jax import lax
from jax.experimental import pallas as pl
from jax.experimental.pallas import tpu as pltpu

F32 = jnp.float32
BF16 = jnp.bfloat16

D_MODEL = 1024
SSD_WIDTH = 512
SSD_HEADDIM = 64
SSD_HEADS = 8
SSD_GROUPS = 2
SSD_HEADS_PER_GROUP = 4
SSD_STATE = 128
SSD_XBC = SSD_WIDTH + 2 * SSD_GROUPS * SSD_STATE
CONV_WIDTH = 4
CHUNK = 128
DIFF_HEAD_DIM = 64
DIFF_HEADS = 4
DIFF_V_DIM = 128
ATTN_QK = 512
ATTN_V = 512
ROPE_THETA = 10000.0
D_FF = 2816
N_MOD = 6
EPS = 1e-5

LANES = 128
SUBLANES = 8
VMEM_LIMIT_BYTES = 56 * 1024 * 1024

ZX_W = SSD_WIDTH + SSD_XBC
DT_LANE0 = LANES - SSD_HEADS
Q_OFF_B = LANES
K_OFF_B = Q_OFF_B + ATTN_QK
V_OFF_B = K_OFF_B + ATTN_QK
WB_W = V_OFF_B + ATTN_V

BF16_SUBLANES = 16
VT_ROWS = DIFF_V_DIM + BF16_SUBLANES
LOG2E = math.log2(math.e)

ROW_TILE = 512
ATTN_K_TILE = 512
ATTN_Q_TILE = 2 * ATTN_K_TILE
FF_TILE = 2816
FFN_ROW_TILE = 1024
FFN_ROW_SPLIT = 4


def _sigmoid(x):
    return 1.0 / (1.0 + jnp.exp(-x))


def _silu(x):
    hx = 0.5 * x
    return hx + hx * jnp.tanh(hx)


def _params(*sem):
    return pltpu.CompilerParams(dimension_semantics=sem, vmem_limit_bytes=VMEM_LIMIT_BYTES)


def _mod_kernel(c_ref, w_ref, b_ref, o_ref):
    c = c_ref[...]
    cond = (c * _sigmoid(c)).astype(BF16)
    o_ref[0] = jnp.dot(cond, w_ref[0].astype(BF16), preferred_element_type=F32) + b_ref[0]


def _modulation(c_pad, w_mod, b_mod):
    depth, d, n = w_mod.shape
    tn = 2048
    return pl.pallas_call(
        _mod_kernel,
        grid=(depth, n // tn),
        in_specs=[
            pl.BlockSpec((SUBLANES, d), lambda l, j: (0, 0)),
            pl.BlockSpec((1, d, tn), lambda l, j: (l, 0, j)),
            pl.BlockSpec((1, 1, tn), lambda l, j: (l, 0, j)),
        ],
        out_specs=pl.BlockSpec((1, SUBLANES, tn), lambda l, j: (l, 0, j)),
        out_shape=jax.ShapeDtypeStruct((depth, SUBLANES, n), F32),
        compiler_params=_params("parallel", "parallel"),
    )(c_pad, w_mod, b_mod.reshape(depth, 1, n))


def _split(x, pieces):
    out = []
    for _ in range(pieces):
        p = x.astype(BF16)
        out.append(p)
        x = x - p.astype(F32)
    return out


def _ssd_chunk(u, z, dt_raw, cw_ref, cb_ref, dtb_ref, alog_ref, dsk_ref, nw_ref, ex_ref, xpad, state):
    L = CHUNK
    dot = functools.partial(jnp.dot, preferred_element_type=F32)

    def taps(v):
        acc = cb_ref[...] + cw_ref[CONV_WIDTH - 1:CONV_WIDTH, :] * v
        for kk in range(CONV_WIDTH - 1):
            acc = acc + cw_ref[kk:kk + 1, :] * pltpu.roll(v, CONV_WIDTH - 1 - kk, 0)
        return acc

    head = taps(jnp.concatenate([xpad[...], u[0:SUBLANES]], axis=0))[SUBLANES:]
    conv = jnp.concatenate([head, taps(u)[SUBLANES:]], axis=0)
    xpad[...] = u[L - SUBLANES:L]
    xc = _silu(conv)
    xs = xc[:, 0:SSD_WIDTH]

    pre = dt_raw + dtb_ref[...]
    dt = jnp.maximum(pre, 0.0) + jnp.log(1.0 + jnp.exp(-jnp.abs(pre)))
    a = dt * (-jnp.exp(alog_ref[...]))
    row = lax.broadcasted_iota(jnp.int32, (L, L), 0)
    col = lax.broadcasted_iota(jnp.int32, (L, L), 1)
    causal = row >= col
    tril = causal.astype(BF16)
    cs = sum(dot(tril, p) for p in _split(a, 3))
    cs_t = cs.T
    expand = lambda v: sum(dot(p, ex_ref[...]) for p in _split(v, 2))
    xdt = xs * expand(dt)
    ecs_x = jnp.exp(expand(cs))
    even_head = (lax.broadcasted_iota(jnp.int32, xdt.shape, 1) // SSD_HEADDIM) % 2 == 0
    xdt_even = jnp.where(even_head, xdt, 0.0).astype(BF16)
    xdt_odd = jnp.where(even_head, 0.0, xdt).astype(BF16)

    ys = []
    for g in range(SSD_GROUPS):
        bm = xc[:, SSD_WIDTH + g * SSD_STATE:SSD_WIDTH + (g + 1) * SSD_STATE]
        cm = xc[:, SSD_WIDTH + (SSD_GROUPS + g) * SSD_STATE:SSD_WIDTH + (SSD_GROUPS + g + 1) * SSD_STATE]
        cm_b = cm.astype(BF16)
        cb = lax.dot_general(cm_b, bm.astype(BF16), (((1,), (1,)), ((), ())),
                             preferred_element_type=F32)
        bm_t = bm.T
        for pp in range(SSD_HEADS_PER_GROUP // 2):
            pr = g * (SSD_HEADS_PER_GROUP // 2) + pp
            halves = (xdt_even[:, pr * LANES:(pr + 1) * LANES],
                      xdt_odd[:, pr * LANES:(pr + 1) * LANES])
            y_pair = None
            st_pair = None
            for half in range(2):
                hd = DT_LANE0 + 2 * pr + half
                cs_col = cs[:, hd:hd + 1]
                cs_row = cs_t[hd:hd + 1, :]
                cs_last = cs_t[hd:hd + 1, L - 1:L]
                lmat = jnp.exp(jnp.where(causal, cs_col - cs_row, -jnp.inf))
                y_h = jnp.dot((cb * lmat).astype(BF16), halves[half], preferred_element_type=F32)
                decay = jnp.exp(cs_last - cs_row)
                st_h = jnp.dot((bm_t * decay).astype(BF16), halves[half],
                               preferred_element_type=F32)
                y_pair = y_h if y_pair is None else y_pair + y_h
                st_pair = st_h if st_pair is None else st_pair + st_h
            prev = state[pr]
            ecs_p = ecs_x[:, pr * LANES:(pr + 1) * LANES]
            y_off = jnp.dot(cm_b, prev.astype(BF16), preferred_element_type=F32) * ecs_p
            state[pr] = ecs_p[L - 1:L, :] * prev + st_pair
            ys.append(y_pair + y_off)

    y = (jnp.concatenate(ys, axis=1) + dsk_ref[...] * xs) * _silu(z)
    gw = SSD_WIDTH // SSD_GROUPS
    out = []
    for g in range(SSD_GROUPS):
        yg = y[:, g * gw:(g + 1) * gw]
        ms = jnp.mean(yg * yg, axis=-1, keepdims=True)
        out.append((yg * lax.rsqrt(ms + EPS) * nw_ref[:, g * gw:(g + 1) * gw]).astype(BF16))
    return jnp.concatenate(out, axis=1)


def _inproj_ssd_kernel(x_ref, sc_ref, sh_ref, wa_ref, wb_ref, cos_ref, sin_ref,
                       cw_ref, cb_ref, dtb_ref, alog_ref, dsk_ref, nw_ref, ex_ref,
                       y_ref, qt_ref, k_ref, vt_ref, xpad, state, *, per_seq):
    @pl.when(pl.program_id(0) % per_seq == 0)
    def _():
        xpad[...] = jnp.zeros_like(xpad)
        state[...] = jnp.zeros_like(state)

    dot = functools.partial(jnp.dot, preferred_element_type=F32)
    h = (x_ref[...] * (1.0 + sc_ref[0]) + sh_ref[0]).astype(BF16)
    cos = cos_ref[...]
    sin = sin_ref[...]
    lane = lax.broadcasted_iota(jnp.int32, cos.shape, 1)
    first_half = (lane % DIFF_HEAD_DIM) < (DIFF_HEAD_DIM // 2)

    def rope(t):
        rot = jnp.where(first_half, pltpu.roll(t, LANES - DIFF_HEAD_DIM // 2, 1),
                        pltpu.roll(t, DIFF_HEAD_DIM // 2, 1))
        return t * cos + rot * sin

    ssd = functools.partial(_ssd_chunk, cw_ref=cw_ref, cb_ref=cb_ref, dtb_ref=dtb_ref,
                            alog_ref=alog_ref, dsk_ref=dsk_ref, nw_ref=nw_ref, ex_ref=ex_ref,
                            xpad=xpad, state=state)

    def ssd_rows(c):
        rows = slice(c * CHUNK, (c + 1) * CHUNK)
        y_ref[rows, :] = ssd(xbc[rows], z[rows], dtq[rows, 0:LANES])

    xbc = dot(h, wa_ref[:, SSD_WIDTH:ZX_W])
    dtq = dot(h, wb_ref[:, 0:K_OFF_B])
    z = dot(h, wa_ref[:, 0:SSD_WIDTH])
    ssd_rows(0)
    k = dot(h, wb_ref[:, K_OFF_B:V_OFF_B])
    ssd_rows(1)
    v = dot(h, wb_ref[:, V_OFF_B:WB_W])
    ssd_rows(2)
    ones = jnp.ones((BF16_SUBLANES, h.shape[0]), BF16)
    for hd in range(DIFF_HEADS):
        lo = hd * LANES
        qh = rope(dtq[:, Q_OFF_B + lo:Q_OFF_B + lo + LANES]) * (LOG2E * DIFF_HEAD_DIM ** -0.5)
        k_ref[:, lo:lo + LANES] = rope(k[:, lo:lo + LANES]).astype(BF16)
        qt_ref[lo:lo + LANES, :] = qh.T.astype(BF16)
        vt_ref[hd * VT_ROWS:hd * VT_ROWS + DIFF_V_DIM, :] = v[:, lo:lo + LANES].T.astype(BF16)
        vt_ref[hd * VT_ROWS + DIFF_V_DIM:(hd + 1) * VT_ROWS, :] = ones
    ssd_rows(3)


def _inproj_ssd(x2, mod3, layer, w_a, w_b, cos2, sin2, conv_w8, conv_b, dtb, alog, dsk, nw, seq):
    t, d = x2.shape
    tm = ROW_TILE
    assert tm == 4 * CHUNK
    per_seq = seq // tm
    row = lambda i: (i, 0)
    const = lambda i: (0, 0)
    head_of_col = jnp.arange(SSD_WIDTH)[None, :] // SSD_HEADDIM
    head = jnp.arange(LANES)[:, None] - DT_LANE0
    expand = (head_of_col == head).astype(BF16)
    return pl.pallas_call(
        functools.partial(_inproj_ssd_kernel, per_seq=per_seq),
        grid=(t // tm,),
        in_specs=[
            pl.BlockSpec((tm, d), row),
            pl.BlockSpec((1, 1, d), lambda i: ((i // per_seq) * N_MOD + 1, 0, 0)),
            pl.BlockSpec((1, 1, d), lambda i: ((i // per_seq) * N_MOD + 0, 0, 0)),
            pl.BlockSpec((None, d, ZX_W), lambda i: (layer, 0, 0)),
            pl.BlockSpec((None, d, WB_W), lambda i: (layer, 0, 0)),
            pl.BlockSpec((tm, LANES), lambda i: (i % per_seq, 0)),
            pl.BlockSpec((tm, LANES), lambda i: (i % per_seq, 0)),
            pl.BlockSpec((SUBLANES, SSD_XBC), const),
            pl.BlockSpec((1, SSD_XBC), const),
            pl.BlockSpec((1, LANES), const),
            pl.BlockSpec((1, LANES), const),
            pl.BlockSpec((1, SSD_WIDTH), const),
            pl.BlockSpec((1, SSD_WIDTH), const),
            pl.BlockSpec((LANES, SSD_WIDTH), const),
        ],
        out_specs=[
            pl.BlockSpec((tm, SSD_WIDTH), row),
            pl.BlockSpec((ATTN_QK, tm), lambda i: (0, i)),
            pl.BlockSpec((tm, ATTN_QK), row),
            pl.BlockSpec((DIFF_HEADS * VT_ROWS, tm), lambda i: (0, i)),
        ],
        out_shape=[
            jax.ShapeDtypeStruct((t, SSD_WIDTH), BF16),
            jax.ShapeDtypeStruct((ATTN_QK, t), BF16),
            jax.ShapeDtypeStruct((t, ATTN_QK), BF16),
            jax.ShapeDtypeStruct((DIFF_HEADS * VT_ROWS, t), BF16),
        ],
        scratch_shapes=[
            pltpu.VMEM((SUBLANES, SSD_XBC), F32),
            pltpu.VMEM((SSD_HEADS // 2, SSD_STATE, 2 * SSD_HEADDIM), F32),
        ],
        compiler_params=_params("arbitrary"),
    )(x2, mod3, mod3, w_a, w_b, cos2, sin2, conv_w8, conv_b, dtb, alog, dsk, nw, expand)


def _attn_kernel(qt_ref, k_ref, vt_ref, lam_ref, nw_ref, o_ref, sa_ref, sb_ref, mxa_ref, mxb_ref,
                 acc_ref, m_ref, qm_ref, *, lambda_init, nq):
    tq, tk = ATTN_Q_TILE, ATTN_K_TILE
    dv = DIFF_V_DIM
    buf_a = (sa_ref, mxa_ref)
    buf_b = (sb_ref, mxb_ref)
    lq = lam_ref[...]
    lam = (jnp.exp(jnp.sum(lq[0:1] * lq[1:2], axis=-1, keepdims=True))
           - jnp.exp(jnp.sum(lq[2:3] * lq[3:4], axis=-1, keepdims=True)) + lambda_init)

    def finalize(blk):
        r0 = 1.0 / acc_ref[0, dv:dv + 1, :]
        r1 = lam / acc_ref[1, dv:dv + 1, :]
        o = acc_ref[0, 0:dv, :] * r0 - acc_ref[1, 0:dv, :] * r1
        ms = jnp.mean(o * o, axis=0, keepdims=True)
        o = o * lax.rsqrt(ms + EPS) * nw_ref[...] * (1.0 - lambda_init)
        o_ref[pl.ds(pl.multiple_of(blk * tq, tq), tq), :] = o.T.astype(BF16)

    quarters = tuple((c * (tk // 2), (c + 1) * (tk // 2)) for c in range(4))
    pieces = tuple((m, lo, hi) for m in range(2) for lo, hi in quarters)

    def scores(j, buf, piece, nk=tk):
        s_ref, mx_ref = buf
        m, lo, hi = piece
        kt = k_ref[pl.ds(pl.multiple_of(j * tk, tk), nk), :]
        s = jnp.dot(kt, qm_ref[m, :, lo:hi], preferred_element_type=F32)
        s_ref[m, 0:nk, lo:hi] = s
        mx_ref[m, :, lo:hi] = jnp.max(s, axis=0, keepdims=True)

    def diag_keys(piece, first_key):
        return min(tk, piece[2] - first_key)

    def consume(j, buf, piece, first_key=None):
        s_ref, mx_ref = buf
        m, lo, hi = piece
        nk = tk if first_key is None else diag_keys(piece, first_key)
        vt = vt_ref[:, pl.ds(pl.multiple_of(j * tk, tk), nk)]
        s = s_ref[m, 0:nk, lo:hi]
        mx = mx_ref[m, :, lo:hi]
        m_old = m_ref[m, :, lo:hi]
        if first_key is not None:
            krow = lax.broadcasted_iota(jnp.int32, s.shape, 0)
            qcol = lax.broadcasted_iota(jnp.int32, s.shape, 1)
            s = jnp.where(krow <= qcol + (lo - first_key), s, -jnp.inf)
            mx = jnp.max(s, axis=0, keepdims=True)
        m_new = jnp.maximum(m_old, mx)
        p = jnp.exp2(s - m_new).astype(BF16)
        acc_ref[m, :, lo:hi] = (jnp.exp2(m_old - m_new) * acc_ref[m, :, lo:hi]
                                + jnp.dot(vt, p, preferred_element_type=F32))
        m_ref[m, :, lo:hi] = m_new

    def start_block(qi):
        qt = qt_ref[:, pl.ds(pl.multiple_of(qi * tq, tq), tq)]
        feat = lax.broadcasted_iota(jnp.int32, qt.shape, 0)
        zero = jnp.zeros_like(qt)
        qm_ref[0] = jnp.where(feat < DIFF_HEAD_DIM, qt, zero)
        qm_ref[1] = jnp.where(feat >= DIFF_HEAD_DIM, qt, zero)
        for pc in pieces:
            scores(0, buf_a, pc)

    start_block(0)

    @pl.loop(0, nq)
    def _(qi):
        acc_ref[...] = jnp.zeros_like(acc_ref)
        m_ref[...] = jnp.full(m_ref.shape, -jnp.inf, F32)

        def tile_pair(j):
            for pc in pieces:
                scores(j + 1, buf_b, pc)
                consume(j, buf_a, pc)
            for pc in pieces:
                scores(j + 2, buf_a, pc)
                consume(j + 1, buf_b, pc)

        @pl.loop(0, qi // 4)
        def _(jj):
            for u in range(4):
                tile_pair(8 * jj + 2 * u)

        done = (qi // 4) * 4

        @pl.when(qi % 4 >= 2)
        def _():
            tile_pair(2 * done)
            tile_pair(2 * done + 2)

        @pl.when(qi % 2 == 1)
        def _():
            tile_pair(2 * qi - 2)

        for m in range(2):
            q0, q1, q2, q3 = ((m,) + q for q in quarters)
            scores(2 * qi + 1, buf_b, q2, nk=diag_keys(q2, tk))
            consume(2 * qi, buf_a, q0, first_key=0)
            scores(2 * qi + 1, buf_b, q3, nk=diag_keys(q3, tk))
            consume(2 * qi, buf_a, q1, first_key=0)
            consume(2 * qi, buf_a, q2)
            consume(2 * qi, buf_a, q3)
        for m in range(2):
            consume(2 * qi + 1, buf_b, (m,) + quarters[2], first_key=tk)
            consume(2 * qi + 1, buf_b, (m,) + quarters[3], first_key=tk)

        finalize(qi)
        start_block(jnp.minimum(qi + 1, nq - 1))


def _attention(qt, k, vt, lam_qk, nw_col, lambda_init, batch, seq):
    t = k.shape[0]
    tq, tk = ATTN_Q_TILE, ATTN_K_TILE
    nq = seq // tq
    return pl.pallas_call(
        functools.partial(_attn_kernel, lambda_init=lambda_init, nq=nq),
        grid=(batch, DIFF_HEADS),
        in_specs=[
            pl.BlockSpec((LANES, seq), lambda b, h: (h, b)),
            pl.BlockSpec((seq, LANES), lambda b, h: (b, h)),
            pl.BlockSpec((VT_ROWS, seq), lambda b, h: (h, b)),
            pl.BlockSpec((4, DIFF_HEAD_DIM), lambda b, h: (0, 0)),
            pl.BlockSpec((DIFF_V_DIM, 1), lambda b, h: (0, 0)),
        ],
        out_specs=pl.BlockSpec((seq, LANES), lambda b, h: (b, h)),
        out_shape=jax.ShapeDtypeStruct((t, ATTN_V), BF16),
        scratch_shapes=[pltpu.VMEM((2, tk, tq), F32), pltpu.VMEM((2, tk, tq), F32),
                        pltpu.VMEM((2, 1, tq), F32), pltpu.VMEM((2, 1, tq), F32),
                        pltpu.VMEM((2, VT_ROWS, tq), F32), pltpu.VMEM((2, 1, tq), F32),
                        pltpu.VMEM((2, LANES, tq), BF16)],
        compiler_params=_params("parallel", "parallel"),
    )(qt, k, vt, lam_qk, nw_col)


def _layer_norm(r, g, b):
    mu = jnp.mean(r, axis=-1, keepdims=True)
    d = r - mu
    var = jnp.mean(d * d, axis=-1, keepdims=True)
    return d * lax.rsqrt(var + EPS) * g + b


def _mix_ffn_kernel(ys_ref, ya_ref, x_ref, g1_ref, sc2_ref, sh2_ref, g2_ref, wo_ref, wgu_ref, wd_ref,
                    n1g_ref, n1b_ref, n2g_ref, n2b_ref, o_ref, *, alpha):
    tm = x_ref.shape[0]
    halves = [pl.ds(s * (tm // FFN_ROW_SPLIT), tm // FFN_ROW_SPLIT) for s in range(FFN_ROW_SPLIT)]
    dot = functools.partial(jnp.dot, preferred_element_type=F32)

    def outproj(rows):
        return (dot(ys_ref[rows, :], wo_ref[0:SSD_WIDTH, :])
                + dot(ya_ref[rows, :], wo_ref[SSD_WIDTH:D_MODEL, :]))

    def norm_modulate(y, rows):
        x1 = _layer_norm(alpha * x_ref[rows, :] + (1.0 + g1_ref[0]) * y, n1g_ref[...], n1b_ref[...])
        return x1, (x1 * (1.0 + sc2_ref[0]) + sh2_ref[0]).astype(BF16)

    def swiglu(h):
        acc = None
        for c in range(D_FF // FF_TILE):
            lo = c * FF_TILE
            gt = dot(h, wgu_ref[:, lo:lo + FF_TILE])
            up = dot(h, wgu_ref[:, D_FF + lo:D_FF + lo + FF_TILE])
            part = dot((_silu(gt) * up).astype(BF16), wd_ref[lo:lo + FF_TILE, :])
            acc = part if acc is None else acc + part
        return acc

    ys = [outproj(rows) for rows in halves]
    outs = []
    for y, rows in zip(ys, halves):
        x1, h = norm_modulate(y, rows)
        outs.append((x1, swiglu(h)))
    for (x1, acc), rows in zip(outs, halves):
        o_ref[rows, :] = _layer_norm(alpha * x1 + (1.0 + g2_ref[0]) * acc, n2g_ref[...], n2b_ref[...])


def _mix_ffn(ys, ya, x2, mod3, layer, w_out, w_gu, w_down, n1g, n1b, n2g, n2b, alpha, seq):
    t, d = x2.shape
    tm = FFN_ROW_TILE
    per_seq = seq // tm
    row = lambda i: (i, 0)
    const = lambda i: (0, 0)
    mod_row = lambda k: pl.BlockSpec((1, 1, d), lambda i: ((i // per_seq) * N_MOD + k, 0, 0))
    resident = lambda shape: pl.BlockSpec((None,) + shape, lambda i: (layer, 0, 0),
                                          pipeline_mode=pl.Buffered(1))
    return pl.pallas_call(
        functools.partial(_mix_ffn_kernel, alpha=alpha),
        grid=(t // tm,),
        in_specs=[
            pl.BlockSpec((tm, SSD_WIDTH), row),
            pl.BlockSpec((tm, ATTN_V), row),
            pl.BlockSpec((tm, d), row),
            mod_row(2), mod_row(4), mod_row(3), mod_row(5),
            resident((d, d)),
            resident((d, 2 * D_FF)),
            resident((D_FF, d)),
            pl.BlockSpec((1, d), const), pl.BlockSpec((1, d), const),
            pl.BlockSpec((1, d), const), pl.BlockSpec((1, d), const),
        ],
        out_specs=pl.BlockSpec((tm, d), row),
        out_shape=jax.ShapeDtypeStruct((t, d), F32),
        compiler_params=_params("parallel"),
    )(ys, ya, x2, mod3, mod3, mod3, mod3, w_out, w_gu, w_down, n1g, n1b, n2g, n2b)


def _rope_tables(seq):
    dim = DIFF_HEAD_DIM
    inv = 1.0 / (ROPE_THETA ** (jnp.arange(0, dim, 2, dtype=F32) / dim))
    ang = jnp.arange(seq, dtype=F32)[:, None] * inv[None, :]
    cos, sin = jnp.cos(ang), jnp.sin(ang)
    return (jnp.concatenate([cos, cos, cos, cos], -1), jnp.concatenate([-sin, sin, -sin, sin], -1))


def _pack_w_in(w):
    w_a = w.astype(BF16)
    w_b = jnp.pad(w_a[..., ZX_W:], ((0, 0), (0, 0), (DT_LANE0, 0)))
    return w_a, w_b


def _dt_lanes(v):
    return jnp.pad(v, (DT_LANE0, 0)).reshape(1, LANES)


def kernel(x, c, w_mod, b_mod, w_in, conv_w, conv_b, dt_bias, a_log, d_skip, ssd_norm_w, lam_qk,
           attn_norm_w, w_out, ln1_g, ln1_b, w_gate_up, w_down, ln2_g, ln2_b):
    batch, seq, d = x.shape
    depth = w_mod.shape[0]
    assert d == D_MODEL and seq % ROW_TILE == 0 and seq % ATTN_Q_TILE == 0 and batch <= SUBLANES
    t = batch * seq
    alpha = (2 * depth) ** 0.25

    cos2, sin2 = _rope_tables(seq)
    c_pad = jnp.pad(c, ((0, SUBLANES - batch), (0, 0)))
    mod = _modulation(c_pad, w_mod, b_mod)
    w_in_a, w_in_b = _pack_w_in(w_in)
    w_out_b, w_gu_b, w_down_b = w_out.astype(BF16), w_gate_up.astype(BF16), w_down.astype(BF16)

    x2 = x.reshape(t, d)
    for l in range(depth):
        lambda_init = 0.8 - 0.6 * math.exp(-0.3 * l)
        mod3 = mod[l, :batch].reshape(batch * N_MOD, 1, d)
        y_ssd, q, k, vt = _inproj_ssd(
            x2, mod3, l, w_in_a, w_in_b, cos2, sin2,
            jnp.pad(conv_w[l], ((0, SUBLANES - CONV_WIDTH), (0, 0))),
            conv_b[l].reshape(1, SSD_XBC), _dt_lanes(dt_bias[l]), _dt_lanes(a_log[l]),
            jnp.repeat(d_skip[l], SSD_HEADDIM).reshape(1, SSD_WIDTH),
            ssd_norm_w[l].reshape(1, SSD_WIDTH), seq)
        y_attn = _attention(q, k, vt, lam_qk[l], attn_norm_w[l].reshape(DIFF_V_DIM, 1),
                            lambda_init, batch, seq)
        x2 = _mix_ffn(y_ssd, y_attn, x2, mod3, l, w_out_b, w_gu_b, w_down_b,
                      ln1_g[l].reshape(1, d), ln1_b[l].reshape(1, d),
                      ln2_g[l].reshape(1, d), ln2_b[l].reshape(1, d), alpha, seq)
    return x2.reshape(batch, seq, d)
```
